```python
import math
import jax, jax.numpy as jnp
from jax import lax
import numpy as np

D_MODEL = 2048
BATCH = 2
SEQ = 4096
DEPTH = 4
DEC_BATCH = 8
DEC_SEQ = 1
PAST_LEN = 16384
PAGE_SIZE = 128

N_A_LAYERS = DEPTH // 2
N_B_LAYERS = DEPTH - N_A_LAYERS
HG_HEADS = 16
HG_DK = D_MODEL // HG_HEADS
HG_DV = D_MODEL // HG_HEADS
HG_CHUNK = 32
DIL_GROUPS = ((128, 1), (512, 4), (2048, 16))
N_GROUPS = len(DIL_GROUPS)
DIL_HEADS = 8
DIL_HD = 128
D_FF = 5632
EPS = 1e-6
NEG = -1e30
f32 = jnp.float32

kernel_name = "yoco_hgrn2_dilated_swa_decoder_step"


def rmsnorm(x, g):
    xf = x.astype(f32)
    y = xf * lax.rsqrt(jnp.mean(xf * xf, axis=-1, keepdims=True) + EPS)
    return (y * g.astype(f32)).astype(x.dtype)


def head_rmsnorm(x, g):
    xf = x.astype(f32)
    return xf * lax.rsqrt(jnp.mean(xf * xf, axis=-1, keepdims=True) + EPS) * g.astype(f32)


def swiglu(x, w_in, w_out):
    gate, up = jnp.split(x @ w_in, 2, axis=-1)
    return (jax.nn.silu(gate) * up) @ w_out


def hgrn_lower_bounds(lb_logits):
    p = jax.nn.softmax(lb_logits.astype(f32), axis=0)
    return jnp.cumsum(p, axis=0) - p[0]


def hgrn_chunked(q, log_f, k, v, s0):
    b, L, h, _ = q.shape
    c = min(HG_CHUNK, L)
    pad = (-L) % c
    padw = ((0, 0), (0, pad), (0, 0), (0, 0))
    q, log_f, k, v = (jnp.pad(a, padw) for a in (q, log_f, k, v))
    n = (L + pad) // c

    def to_chunks(a):
        return jnp.moveaxis(a.reshape(b, n, c, h, a.shape[-1]), 1, 0)

    causal = jnp.tril(jnp.ones((c, c), dtype=bool))[None, :, :, None, None]

    def step(state, inp):
        qc, lfc, kc, vc = inp
        cum = jnp.cumsum(lfc, axis=1)
        diff = jnp.where(causal, cum[:, :, None] - cum[:, None, :], 0.0)
        decay = jnp.where(causal, jnp.exp(diff), 0.0)
        att = jnp.einsum('bthd,btshd,bshd->bhts', qc, decay, kc)
        o = (jnp.einsum('bhts,bshv->bthv', att, vc)
             + jnp.einsum('bthd,bhdv->bthv', qc * jnp.exp(cum), state))
        last = cum[:, -1]
        state = (jnp.exp(last)[..., None] * state
                 + jnp.einsum('bshd,bshv->bhdv', kc * jnp.exp(last[:, None] - cum), vc))
        return state, o

    state, o = lax.scan(step, s0, tuple(to_chunks(a) for a in (q, log_f, k, v)))
    o = jnp.moveaxis(o, 0, 1).reshape(b, n * c, h, v.shape[-1])[:, :L]
    return o, state


def hgrn2_mixer(x, w_in, w_out, lb, g_out, s0):
    b, L, _ = x.shape
    q, fz, i, g = jnp.split(x @ w_in, 4, axis=-1)
    heads = lambda a: a.reshape(b, L, HG_HEADS, -1).astype(f32)
    lbh = lb.reshape(HG_HEADS, HG_DK)
    fz = heads(fz)
    k = (1.0 - lbh) * jax.nn.sigmoid(-fz)
    f = lbh + (1.0 - lbh) * jax.nn.sigmoid(fz)
    log_f = jnp.log(jnp.maximum(f, 1e-30))
    o, s_new = hgrn_chunked(heads(q), log_f, k, heads(i), s0.astype(f32))
    o = head_rmsnorm(o, g_out) * jax.nn.silu(heads(g))
    return o.reshape(b, L, D_MODEL).astype(x.dtype) @ w_out, s_new


def alibi_slopes(dil):
    return jnp.asarray((2.0 ** -(np.arange(DIL_HEADS) + 1.0)) / dil, dtype=f32)


def dilated_group_prompt(q, k, v, win, dil):
    b, S, H, E = q.shape
    n = win // dil
    Ls = S // dil

    def by_residue(a):
        return jnp.swapaxes(a.reshape(b, Ls, dil, H, E), 1, 2)

    qr, kr, vr = by_residue(q), by_residue(k), by_residue(v)
    pad = (-Ls) % n
    nb = (Ls + pad) // n
    qr = jnp.pad(qr, ((0, 0), (0, 0), (0, pad), (0, 0), (0, 0))).reshape(b, dil, nb, n, H, E)

    def banded(a):
        a = jnp.pad(a, ((0, 0), (0, 0), (n, pad), (0, 0), (0, 0))).reshape(b, dil, nb + 1, n, H, E)
        return jnp.concatenate([a[:, :, :-1], a[:, :, 1:]], axis=3)

    kb, vb = banded(kr), banded(vr)
    s = jnp.einsum('brcqhe,brckhe->brchqk', qr, kb)
    steps = n + jnp.arange(n)[:, None] - jnp.arange(2 * n)[None, :]
    key_u = (jnp.arange(nb)[:, None, None] - 1) * n + jnp.arange(2 * n)[None, None, :]
    valid = (steps[None] >= 0) & (steps[None] <= n) & (key_u >= 0)
    bias = -alibi_slopes(dil)[:, None, None] * (steps * dil).astype(f32)[None]
    s = jnp.where(valid[None, None, :, None], s + bias[None, None, None], NEG)
    m = jax.lax.stop_gradient(jnp.max(s, axis=-1, keepdims=True))
    p = jnp.where(valid[None, None, :, None], jnp.exp(s - m), 0.0)
    l = jnp.sum(p, axis=-1, keepdims=True)
    o = jnp.einsum('brchqk,brckhe->brcqhe', p, vb) / jnp.swapaxes(l, 3, 4)
    lse = jnp.swapaxes((m + jnp.log(l))[..., 0], 3, 4)
    o = jnp.swapaxes(o.reshape(b, dil, nb * n, H, E)[:, :, :Ls], 1, 2).reshape(b, S, H, E)
    lse = jnp.swapaxes(lse.reshape(b, dil, nb * n, H)[:, :, :Ls], 1, 2).reshape(b, S, H)
    return o, lse


def dilated_group_sample(q, k_buf, v_buf, k_new, v_new, win, dil):
    Wb, Ln = k_buf.shape[1], q.shape[1]
    n = win // dil
    k_all = jnp.concatenate([k_buf, k_new], axis=1)
    v_all = jnp.concatenate([v_buf, v_new], axis=1)
    steps = jnp.arange(n + 1)
    idx = Wb + jnp.arange(Ln)[:, None] - steps[None] * dil
    valid = idx >= 0
    idx = jnp.maximum(idx, 0)
    kg, vg = k_all[:, idx], v_all[:, idx]
    s = jnp.einsum('bqhe,bqjhe->bhqj', q, kg)
    s = s - alibi_slopes(dil)[:, None, None] * (steps * dil).astype(f32)[None, None]
    s = jnp.where(valid[None, None], s, NEG)
    m = jax.lax.stop_gradient(jnp.max(s, axis=-1, keepdims=True))
    p = jnp.where(valid[None, None], jnp.exp(s - m), 0.0)
    l = jnp.sum(p, axis=-1, keepdims=True)
    o = jnp.einsum('bhqj,bqjhe->bqhe', p, vg) / jnp.swapaxes(l, 1, 2)
    lse = jnp.swapaxes((m + jnp.log(l))[..., 0], 1, 2)
    return o, lse


def shared_kv(h, kv_norm_g, w_kv, k_norm_g):
    b, L, _ = h.shape
    kv = (rmsnorm(h, kv_norm_g) @ w_kv).reshape(b, L, N_GROUPS, 2, DIL_HEADS, DIL_HD)
    k = head_rmsnorm(kv[:, :, :, 0], k_norm_g[:, None, :]).astype(h.dtype)
    return jnp.stack([k, kv[:, :, :, 1]], axis=3)


def dilated_mixer(h, kv, kv_bufs, w_q, q_norm_g, w_o):
    b, L, _ = h.shape
    q = (h @ w_q).reshape(b, L, N_GROUPS, DIL_HEADS, DIL_HD)
    q = head_rmsnorm(q, q_norm_g[:, None, :]) * (DIL_HD ** -0.5)
    kvf = kv.astype(f32)
    outs, lses = [], []
    for gi, (win, dil) in enumerate(DIL_GROUPS):
        if kv_bufs is None:
            o, lse = dilated_group_prompt(q[:, :, gi], kvf[:, :, gi, 0], kvf[:, :, gi, 1], win, dil)
        else:
            buf = kv_bufs[gi].astype(f32)
            o, lse = dilated_group_sample(q[:, :, gi], buf[:, :, 0], buf[:, :, 1],
                                          kvf[:, :, gi, 0], kvf[:, :, gi, 1], win, dil)
        outs.append(o)
        lses.append(lse)
    w = jax.nn.softmax(jnp.stack(lses, axis=0), axis=0)
    o = jnp.sum(jnp.stack(outs, axis=0) * w[..., None], axis=0)
    return o.reshape(b, L, DIL_HEADS * DIL_HD).astype(h.dtype) @ w_o


def trunk(x, hg_s0, kv_bufs, norm_g, ffn_w_in, ffn_w_out, hg_w_in, hg_w_out, hg_lb_logits,
          hg_out_g, kv_norm_g, w_kv, k_norm_g, dil_w_q, dil_q_norm_g, dil_w_o):
    lbs = hgrn_lower_bounds(hg_lb_logits)
    hg_states = []
    kv = None
    for layer in range(DEPTH):
        x = x + 0.5 * swiglu(rmsnorm(x, norm_g[layer, 0]), ffn_w_in[layer, 0], ffn_w_out[layer, 0])
        h = rmsnorm(x, norm_g[layer, 1])
        if layer < N_A_LAYERS:
            y, s = hgrn2_mixer(h, hg_w_in[layer], hg_w_out[layer], lbs[layer], hg_out_g[layer], hg_s0[layer])
            hg_states.append(s)
        else:
            bi = layer - N_A_LAYERS
            y = dilated_mixer(h, kv, kv_bufs, dil_w_q[bi], dil_q_norm_g[bi], dil_w_o[bi])
        x = x + y
        x = x + 0.5 * swiglu(rmsnorm(x, norm_g[layer, 2]), ffn_w_in[layer, 1], ffn_w_out[layer, 1])
        if layer == N_A_LAYERS - 1:
            kv = shared_kv(x, kv_norm_g, w_kv, k_norm_g)
    return x, jnp.stack(hg_states, axis=0), kv


def setup_inputs(seed: int = 0) -> dict:
    key = jax.random.key(seed)
    ks = jax.random.split(key, 24)
    nrm = lambda k, shape, scale: scale * jax.random.normal(k, shape, dtype=f32)
    gain = lambda k, shape: 1.0 + 0.02 * jax.random.normal(k, shape, dtype=f32)
    D = D_MODEL
    inputs = {}
    inputs["x_prompt"] = nrm(ks[0], (BATCH, SEQ, D), 1.0)
    inputs["x_sample"] = nrm(ks[1], (DEC_BATCH, DEC_SEQ, D), 1.0)
    inputs["state_hgrn"] = nrm(ks[2], (N_A_LAYERS, DEC_BATCH, HG_HEADS, HG_DK, HG_DV), 0.5)
    inputs["cache_kv_w128"] = nrm(ks[3], (DEC_BATCH, min(DIL_GROUPS[0][0], PAST_LEN), 2, DIL_HEADS, DIL_HD), 1.0)
    inputs["cache_kv_w512"] = nrm(ks[4], (DEC_BATCH, min(DIL_GROUPS[1][0], PAST_LEN), 2, DIL_HEADS, DIL_HD), 1.0)
    inputs["cache_kv_w2048"] = nrm(ks[5], (DEC_BATCH, min(DIL_GROUPS[2][0], PAST_LEN), 2, DIL_HEADS, DIL_HD), 1.0)
    inputs["norm_g"] = gain(ks[6], (DEPTH, 3, D))
    inputs["ffn_w_in"] = nrm(ks[7], (DEPTH, 2, D, 2 * D_FF), D ** -0.5)
    inputs["ffn_w_out"] = nrm(ks[8], (DEPTH, 2, D_FF, D), D_FF ** -0.5)
    inputs["hg_w_in"] = nrm(ks[9], (N_A_LAYERS, D, 4 * D), D ** -0.5)
    inputs["hg_w_out"] = nrm(ks[10], (N_A_LAYERS, D, D), D ** -0.5)
    inputs["hg_lb_logits"] = nrm(ks[11], (N_A_LAYERS, HG_HEADS * HG_DK), 0.5)
    inputs["hg_out_g"] = gain(ks[12], (N_A_LAYERS, HG_DV))
    inputs["kv_norm_g"] = gain(ks[13], (D,))
    inputs["w_kv"] = nrm(ks[14], (D, N_GROUPS * 2 * DIL_HEADS * DIL_HD), D ** -0.5)
    inputs["k_norm_g"] = gain(ks[15], (N_GROUPS, DIL_HD))
    inputs["dil_w_q"] = nrm(ks[16], (N_B_LAYERS, D, N_GROUPS * DIL_HEADS * DIL_HD), D ** -0.5)
    inputs["dil_q_norm_g"] = gain(ks[17], (N_B_LAYERS, N_GROUPS, DIL_HD))
    inputs["dil_w_o"] = nrm(ks[18], (N_B_LAYERS, DIL_HEADS * DIL_HD, D), (DIL_HEADS * DIL_HD) ** -0.5)
    return inputs


def reference(x_prompt, x_sample, state_hgrn, cache_kv_w128, cache_kv_w512, cache_kv_w2048,
              norm_g, ffn_w_in, ffn_w_out, hg_w_in, hg_w_out, hg_lb_logits, hg_out_g,
              kv_norm_g, w_kv, k_norm_g, dil_w_q, dil_q_norm_g, dil_w_o):
    weights = (norm_g, ffn_w_in, ffn_w_out, hg_w_in, hg_w_out, hg_lb_logits, hg_out_g,
               kv_norm_g, w_kv, k_norm_g, dil_w_q, dil_q_norm_g, dil_w_o)
    hg0 = jnp.zeros((N_A_LAYERS, x_prompt.shape[0], HG_HEADS, HG_DK, HG_DV), f32)
    y_prompt, hg_prompt, kv_prompt = trunk(x_prompt, hg0, None, *weights)
    y_sample, hg_sample, kv_sample = trunk(
        x_sample, state_hgrn, (cache_kv_w128, cache_kv_w512, cache_kv_w2048), *weights)
    S = x_prompt.shape[1]
    r0 = min(DIL_GROUPS[0][0], S)
    r1 = min(DIL_GROUPS[1][0], S)
    r2 = min(DIL_GROUPS[2][0], S)
    return (y_prompt, y_sample, hg_prompt, hg_sample,
            kv_prompt[:, S - r0:, 0], kv_sample[:, :, 0],
            kv_prompt[:, S - r1:, 1], kv_sample[:, :, 1],
            kv_prompt[:, S - r2:, 2], kv_sample[:, :, 2])
```

```python
import functools

import jax
import jax.numpy as jnp
from jax import lax
from jax.experimental import pallas as pl
from jax.experimental.pallas import tpu as pltpu

D_MODEL = 2048
D_FF = 5632
DEPTH = 4
N_A_LAYERS = 2
HG_HEADS = 16
HG_DK = 128
DIL_GROUPS = ((128, 1), (512, 4), (2048, 16))
N_GROUPS = 3
DIL_HEADS = 8
DIL_HD = 128
DIL_DIM = DIL_HEADS * DIL_HD
BAND = 128
EPS = 1e-6
NEG = -1e30
F32 = jnp.float32
BF16 = jnp.bfloat16

VMEM_LIMIT_BYTES = 56 * 1024 * 1024
HG_CHUNK = 32

_NT = (((1,), (1,)), ((), ()))
_TN = (((0,), (0,)), ((), ()))


def _params(*semantics):
    return pltpu.CompilerParams(dimension_semantics=semantics, vmem_limit_bytes=VMEM_LIMIT_BYTES)


def _rms_scale(x):
    return lax.rsqrt(jnp.mean(x * x, axis=-1, keepdims=True) + EPS)


def _silu(x):
    return x / (1.0 + jnp.exp(-x))


def _ffn_kernel(x_ref, g_ref, wg_ref, wu_ref, wo_ref, o_ref, xn_ref, acc_ref):
    j = pl.program_id(1)

    @pl.when(j == 0)
    def _():
        x = x_ref[...]
        xn_ref[...] = (x * _rms_scale(x) * g_ref[...]).astype(BF16)
        acc_ref[...] = jnp.zeros_like(acc_ref)

    xn = xn_ref[...]
    gate = jnp.dot(xn, wg_ref[...], preferred_element_type=F32)
    up = jnp.dot(xn, wu_ref[...], preferred_element_type=F32)
    act = (_silu(gate) * up).astype(BF16)
    acc_ref[...] += jnp.dot(act, wo_ref[...], preferred_element_type=F32)

    @pl.when(j == pl.num_programs(1) - 1)
    def _():
        o_ref[...] = x_ref[...] + 0.5 * acc_ref[...]


def _ffn(x, norm_g, w_in, w_out, layer, half, gain_idx, tm, tf=512):
    m = x.shape[0]
    nj = D_FF // tf
    return pl.pallas_call(
        _ffn_kernel,
        grid=(m // tm, nj),
        in_specs=[
            pl.BlockSpec((tm, D_MODEL), lambda i, j: (i, 0)),
            pl.BlockSpec((None, None, 1, D_MODEL), lambda i, j: (layer, gain_idx, 0, 0)),
            pl.BlockSpec((None, None, D_MODEL, tf), lambda i, j: (layer, half, 0, j)),
            pl.BlockSpec((None, None, D_MODEL, tf), lambda i, j: (layer, half, 0, j + nj)),
            pl.BlockSpec((None, None, tf, D_MODEL), lambda i, j: (layer, half, j, 0)),
        ],
        out_specs=pl.BlockSpec((tm, D_MODEL), lambda i, j: (i, 0)),
        out_shape=jax.ShapeDtypeStruct((m, D_MODEL), F32),
        scratch_shapes=[pltpu.VMEM((tm, D_MODEL), BF16), pltpu.VMEM((tm, D_MODEL), F32)],
        compiler_params=_params("parallel", "arbitrary"),
        name="ffn",
    )(x, norm_g, w_in, w_in, w_out)


def _head_norm_store(y, gain, scale, o_ref):
    for h in range(DIL_HEADS):
        sl = slice(h * DIL_HD, (h + 1) * DIL_HD)
        yh = y[:, sl]
        o_ref[:, sl] = yh * _rms_scale(yh) * gain * scale


def _norm_matmul_kernel(x_ref, g_ref, w_ref, hg_ref, o_ref, xn_ref, *, mode):
    j = pl.program_id(1)

    @pl.when(j == 0)
    def _():
        x = x_ref[...]
        xn_ref[...] = (x * _rms_scale(x) * g_ref[...]).astype(BF16)

    y = jnp.dot(xn_ref[...], w_ref[...], preferred_element_type=F32)
    if mode == "plain":
        o_ref[...] = y
    elif mode == "q":
        _head_norm_store(y, hg_ref[pl.ds(j, 1), :], DIL_HD ** -0.5, o_ref)
    else:
        @pl.when(j % 2 == 0)
        def _():
            _head_norm_store(y, hg_ref[pl.ds(j // 2, 1), :], 1.0, o_ref)

        @pl.when(j % 2 == 1)
        def _():
            o_ref[...] = y


def _norm_matmul(x, gain_arr, gain_spec, w_arr, w_spec, n, head_gain, mode, tm, tn):
    m = x.shape[0]
    return pl.pallas_call(
        functools.partial(_norm_matmul_kernel, mode=mode),
        grid=(m // tm, n // tn),
        in_specs=[
            pl.BlockSpec((tm, D_MODEL), lambda i, j: (i, 0)),
            gain_spec,
            w_spec,
            pl.BlockSpec(head_gain.shape, lambda i, j: (0, 0)),
        ],
        out_specs=pl.BlockSpec((tm, tn), lambda i, j: (i, j)),
        out_shape=jax.ShapeDtypeStruct((m, n), F32),
        scratch_shapes=[pltpu.VMEM((tm, D_MODEL), BF16)],
        compiler_params=_params("parallel", "arbitrary"),
        name="norm_matmul_" + mode,
    )(x, gain_arr, w_arr, head_gain)


def _matmul_res_kernel(a_ref, w_ref, r_ref, o_ref):
    o_ref[...] = r_ref[...] + jnp.dot(a_ref[...].astype(BF16), w_ref[...], preferred_element_type=F32)


def _matmul_res(a, w_arr, layer, res, tm, tn):
    m, k = a.shape
    n = res.shape[1]
    return pl.pallas_call(
        _matmul_res_kernel,
        grid=(m // tm, n // tn),
        in_specs=[
            pl.BlockSpec((tm, k), lambda i, j: (i, 0)),
            pl.BlockSpec((None, k, tn), lambda i, j: (layer, 0, j)),
            pl.BlockSpec((tm, tn), lambda i, j: (i, j)),
        ],
        out_specs=pl.BlockSpec((tm, tn), lambda i, j: (i, j)),
        out_shape=jax.ShapeDtypeStruct((m, n), F32),
        compiler_params=_params("parallel", "arbitrary"),
        name="matmul_res",
    )(a, w_arr, res)


def _hgrn_gates(z, lb):
    e = jnp.exp(-jnp.abs(z))
    inv = 1.0 / (1.0 + e)
    pos = z >= 0
    sig = jnp.where(pos, inv, e * inv)
    sig_neg = jnp.where(pos, e * inv, inv)
    k = (1.0 - lb) * sig_neg
    f = lb + (1.0 - lb) * sig
    return f, k, jnp.log(jnp.maximum(f, 1e-30))


def _hgrn_scan_kernel(q_ref, z_ref, v_ref, gt_ref, lb_ref, go_ref, o_ref, s_ref, st_ref):
    c = HG_CHUNK
    n_sub = c // 8
    h = pl.program_id(1)
    lb = lb_ref[pl.ds(h, 1), :]
    g_out = go_ref[...]
    row = lax.broadcasted_iota(jnp.int32, (c, c), 0)
    col = lax.broadcasted_iota(jnp.int32, (c, c), 1)
    tri = (col <= row).astype(F32)
    sub_row = lax.broadcasted_iota(jnp.int32, (8, 1), 0)
    st_ref[...] = jnp.zeros_like(st_ref)

    def body(ci, carry):
        r0 = pl.multiple_of(ci * c, c)
        q = q_ref[pl.ds(r0, c), :]
        v = v_ref[pl.ds(r0, c), :]
        _, k, lf = _hgrn_gates(z_ref[pl.ds(r0, c), :], lb)
        cum = jnp.dot(tri, lf, preferred_element_type=F32, precision=lax.Precision.HIGHEST)

        acc = [jnp.zeros((8, HG_DK), F32) for _ in range(n_sub)]
        for s in range(c):
            cs = cum[s:s + 1, :]
            ks = k[s:s + 1, :]
            vs = v[s:s + 1, :]
            for tg in range(s // 8, n_sub):
                rows = slice(tg * 8, (tg + 1) * 8)
                p = q[rows, :] * ks * jnp.exp(cum[rows, :] - cs)
                a = jnp.sum(p, axis=-1, keepdims=True)
                if tg == s // 8:
                    a = jnp.where(sub_row >= (s - tg * 8), a, 0.0)
                acc[tg] = acc[tg] + a * vs
        o = jnp.concatenate(acc, axis=0)

        st = st_ref[...]
        qd = (q * jnp.exp(cum)).astype(BF16)
        o = o + lax.dot_general(qd, st.astype(BF16), _NT, preferred_element_type=F32)
        last = cum[c - 1:c, :]
        kd = (k * jnp.exp(last - cum)).astype(BF16)
        st_ref[...] = st * jnp.exp(last) + lax.dot_general(
            v.astype(BF16), kd, _TN, preferred_element_type=F32)

        o = o * _rms_scale(o) * g_out * _silu(gt_ref[pl.ds(r0, c), :])
        o_ref[pl.ds(r0, c), :] = o.astype(o_ref.dtype)
        return carry

    lax.fori_loop(0, q_ref.shape[0] // c, body, 0)
    s_ref[...] = st_ref[...].T


def _hgrn_scan(proj, lb, g_out, batch, seq):
    nh = HG_HEADS
    col = lambda k: (lambda b, h: (b, k * nh + h))
    return pl.pallas_call(
        _hgrn_scan_kernel,
        grid=(batch, nh),
        in_specs=[
            pl.BlockSpec((seq, HG_DK), col(0)),
            pl.BlockSpec((seq, HG_DK), col(1)),
            pl.BlockSpec((seq, HG_DK), col(2)),
            pl.BlockSpec((seq, HG_DK), col(3)),
            pl.BlockSpec((nh, HG_DK), lambda b, h: (0, 0)),
            pl.BlockSpec((1, HG_DK), lambda b, h: (0, 0)),
        ],
        out_specs=[
            pl.BlockSpec((seq, HG_DK), lambda b, h: (b, h)),
            pl.BlockSpec((None, None, HG_DK, HG_DK), lambda b, h: (b, h, 0, 0)),
        ],
        out_shape=[
            jax.ShapeDtypeStruct((batch * seq, D_MODEL), BF16),
            jax.ShapeDtypeStruct((batch, nh, HG_DK, HG_DK), F32),
        ],
        scratch_shapes=[pltpu.VMEM((HG_DK, HG_DK), F32)],
        compiler_params=_params("parallel", "parallel"),
        name="hgrn_scan",
    )(proj, proj, proj, proj, lb, g_out)


def _hgrn_step_kernel(x_ref, s0_ref, lb_ref, go_ref, o_ref, s_ref):
    nh = HG_HEADS
    x = x_ref[...]
    q = x[0:nh]
    f, k, _ = _hgrn_gates(x[nh:2 * nh], lb_ref[...])
    r = lax.broadcasted_iota(jnp.int32, (HG_DK, HG_DK), 0)
    c = lax.broadcasted_iota(jnp.int32, (HG_DK, HG_DK), 1)
    eye = (r == c).astype(F32)
    cols = lax.dot_general(eye, jnp.concatenate([q, f, k], axis=0), _NT,
                           preferred_element_type=F32, precision=lax.Precision.HIGHEST)
    g_out = go_ref[...]
    for h in range(nh):
        qc = cols[:, h:h + 1]
        fc = cols[:, nh + h:nh + h + 1]
        kc = cols[:, 2 * nh + h:2 * nh + h + 1]
        v = x[2 * nh + h:2 * nh + h + 1, :]
        s_new = fc * s0_ref[h] + kc * v
        s_ref[h] = s_new
        o = jnp.sum(s_new * qc, axis=0, keepdims=True)
        o = o * _rms_scale(o) * g_out * _silu(x[3 * nh + h:3 * nh + h + 1, :])
        o_ref[:, h * HG_DK:(h + 1) * HG_DK] = o


def _hgrn_step(proj, state, layer, lb, g_out):
    b = proj.shape[0]
    nh = HG_HEADS
    o, s = pl.pallas_call(
        _hgrn_step_kernel,
        grid=(b,),
        in_specs=[
            pl.BlockSpec((None, 4 * nh, HG_DK), lambda i: (i, 0, 0)),
            pl.BlockSpec((None, None, nh, HG_DK, HG_DK), lambda i: (layer, i, 0, 0, 0)),
            pl.BlockSpec((nh, HG_DK), lambda i: (0, 0)),
            pl.BlockSpec((1, HG_DK), lambda i: (0, 0)),
        ],
        out_specs=[
            pl.BlockSpec((None, 1, D_MODEL), lambda i: (i, 0, 0)),
            pl.BlockSpec((None, nh, HG_DK, HG_DK), lambda i: (i, 0, 0, 0)),
        ],
        out_shape=[
            jax.ShapeDtypeStruct((b, 1, D_MODEL), F32),
            jax.ShapeDtypeStruct((b, nh, HG_DK, HG_DK), F32),
        ],
        compiler_params=_params("parallel"),
        name="hgrn_step",
    )(proj.reshape(b, 4 * nh, HG_DK), state, lb, g_out)
    return o.reshape(b, D_MODEL), s


def _attn_prompt_kernel(q_ref, kc_ref, kp_ref, vc_ref, vp_ref, o_ref, lse_ref):
    ut = pl.program_id(2)
    qi = lax.broadcasted_iota(jnp.int32, (BAND, BAND), 0)
    kj = lax.broadcasted_iota(jnp.int32, (BAND, BAND), 1)
    steps_c = (qi - kj).astype(F32)
    steps_p = steps_c + float(BAND)
    valid_c = kj <= qi
    valid_p = jnp.logical_and(kj >= qi, ut > 0)
    lane = lax.broadcasted_iota(jnp.int32, (BAND, DIL_HD), 1)
    lse_all = jnp.zeros((BAND, DIL_HD), F32)
    for h in range(DIL_HEADS):
        sl = slice(h * DIL_HD, (h + 1) * DIL_HD)
        slope = 2.0 ** -(h + 1)
        q = q_ref[:, sl].astype(BF16)
        sc = lax.dot_general(q, kc_ref[:, sl].astype(BF16), _NT, preferred_element_type=F32)
        sp = lax.dot_general(q, kp_ref[:, sl].astype(BF16), _NT, preferred_element_type=F32)
        sc = jnp.where(valid_c, sc - slope * steps_c, NEG)
        sp = jnp.where(valid_p, sp - slope * steps_p, NEG)
        m = jnp.maximum(jnp.max(sc, axis=-1, keepdims=True), jnp.max(sp, axis=-1, keepdims=True))
        pc = jnp.exp(sc - m)
        pp = jnp.exp(sp - m)
        l = jnp.sum(pc, axis=-1, keepdims=True) + jnp.sum(pp, axis=-1, keepdims=True)
        o = (jnp.dot(pc.astype(BF16), vc_ref[:, sl].astype(BF16), preferred_element_type=F32)
             + jnp.dot(pp.astype(BF16), vp_ref[:, sl].astype(BF16), preferred_element_type=F32))
        o_ref[:, sl] = o / l
        lse_all = jnp.where(lane == h, m + jnp.log(l), lse_all)
    lse_ref[...] = lse_all


def _attn_prompt_group(q_all, kv_all, batch, seq, gi, dil):
    ls = seq // dil
    nt = ls // BAND
    q_view = q_all.reshape(batch, ls, dil * N_GROUPS * DIL_DIM)
    kv_view = kv_all.reshape(batch, ls, dil * 2 * N_GROUPS * DIL_DIM)
    tile = lambda idx: pl.BlockSpec((None, BAND, DIL_DIM), idx)
    kcol = lambda r, kv: r * 2 * N_GROUPS + 2 * gi + kv
    o, lse = pl.pallas_call(
        _attn_prompt_kernel,
        grid=(batch, dil, nt),
        in_specs=[
            tile(lambda b, r, u: (b, u, r * N_GROUPS + gi)),
            tile(lambda b, r, u: (b, u, kcol(r, 0))),
            tile(lambda b, r, u: (b, jnp.maximum(u - 1, 0), kcol(r, 0))),
            tile(lambda b, r, u: (b, u, kcol(r, 1))),
            tile(lambda b, r, u: (b, jnp.maximum(u - 1, 0), kcol(r, 1))),
        ],
        out_specs=[
            pl.BlockSpec((None, BAND, DIL_DIM), lambda b, r, u: (b, u, r)),
            pl.BlockSpec((None, BAND, DIL_HD), lambda b, r, u: (b, u, r)),
        ],
        out_shape=[
            jax.ShapeDtypeStruct((batch, ls, dil * DIL_DIM), F32),
            jax.ShapeDtypeStruct((batch, ls, dil * DIL_HD), F32),
        ],
        compiler_params=_params("parallel", "parallel", "arbitrary"),
        name="attn_prompt_g%d" % gi,
    )(q_view, kv_view, kv_view, kv_view, kv_view)
    return o.reshape(batch * seq, DIL_DIM), lse.reshape(batch * seq, DIL_HD)


def _merge_matmul_res_kernel(o0, o1, o2, l0, l1, l2, w_ref, r_ref, out_ref, a_ref):
    j = pl.program_id(1)

    @pl.when(j == 0)
    def _():
        a0, a1, a2 = l0[...], l1[...], l2[...]
        m = jnp.maximum(jnp.maximum(a0, a1), a2)
        e0, e1, e2 = jnp.exp(a0 - m), jnp.exp(a1 - m), jnp.exp(a2 - m)
        inv = 1.0 / (e0 + e1 + e2)
        w0, w1, w2 = e0 * inv, e1 * inv, e2 * inv
        for h in range(DIL_HEADS):
            sl = slice(h * DIL_HD, (h + 1) * DIL_HD)
            a_ref[:, sl] = (o0[:, sl] * w0[:, h:h + 1] + o1[:, sl] * w1[:, h:h + 1]
                            + o2[:, sl] * w2[:, h:h + 1]).astype(BF16)

    out_ref[...] = r_ref[...] + jnp.dot(a_ref[...], w_ref[...], preferred_element_type=F32)


def _merge_matmul_res(os, lses, w_arr, layer, res, tm, tn):
    m = res.shape[0]
    o_spec = pl.BlockSpec((tm, DIL_DIM), lambda i, j: (i, 0))
    l_spec = pl.BlockSpec((tm, DIL_HD), lambda i, j: (i, 0))
    return pl.pallas_call(
        _merge_matmul_res_kernel,
        grid=(m // tm, D_MODEL // tn),
        in_specs=[o_spec] * 3 + [l_spec] * 3 + [
            pl.BlockSpec((None, DIL_DIM, tn), lambda i, j: (layer, 0, j)),
            pl.BlockSpec((tm, tn), lambda i, j: (i, j)),
        ],
        out_specs=pl.BlockSpec((tm, tn), lambda i, j: (i, j)),
        out_shape=jax.ShapeDtypeStruct((m, D_MODEL), F32),
        scratch_shapes=[pltpu.VMEM((tm, DIL_DIM), BF16)],
        compiler_params=_params("parallel", "arbitrary"),
        name="merge_matmul_res",
    )(*os, *lses, w_arr, res)


def _attn_sample_kernel(q_ref, kvn_ref, c0_ref, c1_ref, c2_ref, o_ref):
    caches = (c0_ref, c1_ref, c2_ref)
    steps = float(BAND) - lax.broadcasted_iota(jnp.int32, (BAND, 1), 0).astype(F32)
    for h in range(DIL_HEADS):
        slope = 2.0 ** -(h + 1)
        outs, lses = [], []
        for g in range(N_GROUPS):
            c_ref = caches[g]
            off = h * DIL_HD
            q = q_ref[:, g * DIL_DIM + off:g * DIL_DIM + off + DIL_HD]
            kb = c_ref[:, off:off + DIL_HD]
            vb = c_ref[:, DIL_DIM + off:DIL_DIM + off + DIL_HD]
            kn = kvn_ref[:, 2 * g * DIL_DIM + off:2 * g * DIL_DIM + off + DIL_HD]
            vn = kvn_ref[:, (2 * g + 1) * DIL_DIM + off:(2 * g + 1) * DIL_DIM + off + DIL_HD]
            sb = jnp.sum(kb * q, axis=-1, keepdims=True) - slope * steps
            sn = jnp.sum(kn * q, axis=-1, keepdims=True)
            m = jnp.maximum(jnp.max(sb, axis=0, keepdims=True), sn)
            pb = jnp.exp(sb - m)
            pn = jnp.exp(sn - m)
            l = jnp.sum(pb, axis=0, keepdims=True) + pn
            outs.append((jnp.sum(pb * vb, axis=0, keepdims=True) + pn * vn) / l)
            lses.append(m + jnp.log(l))
        m = jnp.maximum(jnp.maximum(lses[0], lses[1]), lses[2])
        es = [jnp.exp(a - m) for a in lses]
        inv = 1.0 / (es[0] + es[1] + es[2])
        o_ref[:, h * DIL_HD:(h + 1) * DIL_HD] = (
            outs[0] * (es[0] * inv) + outs[1] * (es[1] * inv) + outs[2] * (es[2] * inv))


def _attn_sample(q, kv_new, caches):
    b = q.shape[0]
    views = []
    for cache, (_, dil) in zip(caches, DIL_GROUPS):
        wb = cache.shape[1]
        views.append(cache.reshape(b, wb // dil, dil * 2 * DIL_DIM))
    row = lambda n: pl.BlockSpec((None, 1, n), lambda i: (i, 0, 0))
    cache_spec = pl.BlockSpec((None, BAND, 2 * DIL_DIM), lambda i: (i, 0, 0))
    o = pl.pallas_call(
        _attn_sample_kernel,
        grid=(b,),
        in_specs=[row(N_GROUPS * DIL_DIM), row(2 * N_GROUPS * DIL_DIM)] + [cache_spec] * 3,
        out_specs=row(DIL_DIM),
        out_shape=jax.ShapeDtypeStruct((b, 1, DIL_DIM), F32),
        compiler_params=_params("parallel"),
        name="attn_sample",
    )(q.reshape(b, 1, -1), kv_new.reshape(b, 1, -1), *views)
    return o.reshape(b, DIL_DIM)


def _trunk(x, batch, seq, hg_state, caches, wts, tm):
    (norm_g, ffn_w_in, ffn_w_out, hg_w_in, hg_w_out, lbs, hg_out_g, kv_norm_g, w_kv,
     k_norm_g, dil_w_q, dil_q_norm_g, dil_w_o) = wts
    prompt = caches is None
    tn = 1024
    hg_states = []
    kv = None
    gain = lambda layer, idx: pl.BlockSpec((None, None, 1, D_MODEL), lambda i, j: (layer, idx, 0, 0))
    for layer in range(DEPTH):
        x = _ffn(x, norm_g, ffn_w_in, ffn_w_out, layer, 0, 0, tm)
        if layer < N_A_LAYERS:
            proj = _norm_matmul(
                x, norm_g, gain(layer, 1), hg_w_in,
                pl.BlockSpec((None, D_MODEL, tn), lambda i, j, layer=layer: (layer, 0, j)),
                4 * D_MODEL, k_norm_g, "plain", tm, tn)
            lb = lbs[layer].reshape(HG_HEADS, HG_DK)
            g_out = hg_out_g[layer].reshape(1, HG_DK)
            if prompt:
                o, s = _hgrn_scan(proj, lb, g_out, batch, seq)
            else:
                o, s = _hgrn_step(proj, hg_state, layer, lb, g_out)
            hg_states.append(s)
            x = _matmul_res(o, hg_w_out, layer, x, tm, tn)
        else:
            bi = layer - N_A_LAYERS
            q = _norm_matmul(
                x, norm_g, gain(layer, 1), dil_w_q,
                pl.BlockSpec((None, D_MODEL, DIL_DIM), lambda i, j, bi=bi: (bi, 0, j)),
                N_GROUPS * DIL_DIM, dil_q_norm_g[bi], "q", tm, DIL_DIM)
            if prompt:
                parts = [_attn_prompt_group(q, kv, batch, seq, gi, dil)
                         for gi, (_, dil) in enumerate(DIL_GROUPS)]
                x = _merge_matmul_res([p[0] for p in parts], [p[1] for p in parts],
                                      dil_w_o, bi, x, tm, tn)
            else:
                o = _attn_sample(q, kv, caches)
                x = _matmul_res(o, dil_w_o, bi, x, tm, tn)
        x = _ffn(x, norm_g, ffn_w_in, ffn_w_out, layer, 1, 2, tm)
        if layer == N_A_LAYERS - 1:
            kv = _norm_matmul(
                x, kv_norm_g.reshape(1, D_MODEL), pl.BlockSpec((1, D_MODEL), lambda i, j: (0, 0)),
                w_kv, pl.BlockSpec((D_MODEL, DIL_DIM), lambda i, j: (0, j)),
                2 * N_GROUPS * DIL_DIM, k_norm_g, "kv", tm, DIL_DIM)
    return x, jnp.stack(hg_states, axis=0), kv


def kernel(x_prompt, x_sample, state_hgrn, cache_kv_w128, cache_kv_w512, cache_kv_w2048,
           norm_g, ffn_w_in, ffn_w_out, hg_w_in, hg_w_out, hg_lb_logits, hg_out_g,
           kv_norm_g, w_kv, k_norm_g, dil_w_q, dil_q_norm_g, dil_w_o):
    batch, seq, _ = x_prompt.shape
    dec_batch = x_sample.shape[0]
    p = jax.nn.softmax(hg_lb_logits.astype(F32), axis=0)
    lbs = jnp.cumsum(p, axis=0) - p[0]
    wts = (norm_g.reshape(DEPTH, 3, 1, D_MODEL), ffn_w_in.astype(BF16), ffn_w_out.astype(BF16),
           hg_w_in.astype(BF16), hg_w_out.astype(BF16), lbs, hg_out_g, kv_norm_g,
           w_kv.astype(BF16), k_norm_g, dil_w_q.astype(BF16), dil_q_norm_g, dil_w_o.astype(BF16))

    y_p, hg_p, kv_p = _trunk(x_prompt.reshape(batch * seq, D_MODEL), batch, seq, None, None, wts, 512)
    y_s, hg_s, kv_s = _trunk(x_sample.reshape(dec_batch, D_MODEL), dec_batch, 1, state_hgrn,
                             (cache_kv_w128, cache_kv_w512, cache_kv_w2048), wts, dec_batch)

    kv_p = kv_p.reshape(batch, seq, N_GROUPS, 2, DIL_HEADS, DIL_HD)
    kv_s = kv_s.reshape(dec_batch, 1, N_GROUPS, 2, DIL_HEADS, DIL_HD)
    outs = [y_p.reshape(batch, seq, D_MODEL), y_s.reshape(dec_batch, 1, D_MODEL), hg_p, hg_s]
    for gi, (win, _) in enumerate(DIL_GROUPS):
        rows = min(win, seq)
        outs.append(kv_p[:, seq - rows:, gi])
        outs.append(kv_s[:, :, gi])
    return tuple(outs)
```

```python
import functools

import jax
import jax.numpy as jnp
from jax import lax
from jax.experimental import pallas as pl
from jax.experimental.pallas import tpu as pltpu

D_MODEL = 2048
D_FF = 5632
DEPTH = 4
N_A_LAYERS = 2
HG_HEADS = 16
HG_DK = 128
DIL_GROUPS = ((128, 1), (512, 4), (2048, 16))
N_GROUPS = 3
DIL_HEADS = 8
DIL_HD = 128
DIL_DIM = DIL_HEADS * DIL_HD
BAND = 128
EPS = 1e-6
NEG = -1e30
F32 = jnp.float32
BF16 = jnp.bfloat16

VMEM_LIMIT_BYTES = 56 * 1024 * 1024
BF16_SUBLANES = 16
HG_CHUNK = 32
HG_TILE = 256
HG_SAFE_DECAY = 80.0

_NT = (((1,), (1,)), ((), ()))
_TN = (((0,), (0,)), ((), ()))


def _params(*semantics):
    return pltpu.CompilerParams(dimension_semantics=semantics, vmem_limit_bytes=VMEM_LIMIT_BYTES)


def _rms_scale(x):
    return lax.rsqrt(jnp.mean(x * x, axis=-1, keepdims=True) + EPS)


def _silu(x):
    return x / (1.0 + jnp.exp(-x))


def _dot(a, b):
    return jnp.dot(a, b, preferred_element_type=F32)


def _ffn_kernel(x_ref, xs_ref, g_ref, wg_ref, wu_ref, wo_ref, o_ref, os_ref, xn_ref, accs_ref):
    j = pl.program_id(1)
    tm = x_ref.shape[0]
    ns = xs_ref.shape[0]

    @pl.when(j == 0)
    def _():
        g = g_ref[...]
        x = x_ref[...]
        xn_ref[0:tm, :] = (x * _rms_scale(x) * g).astype(BF16)
        xs = xs_ref[...]
        xsn = jnp.concatenate(
            [xs * _rms_scale(xs) * g, jnp.zeros((BF16_SUBLANES - ns, D_MODEL), F32)], axis=0)
        xn_ref[tm:tm + BF16_SUBLANES, :] = xsn.astype(BF16)
        o_ref[...] = jnp.zeros_like(o_ref)
        accs_ref[...] = jnp.zeros_like(accs_ref)

    xn = xn_ref[...]
    gate = _dot(xn, wg_ref[...].astype(BF16))
    up = _dot(xn, wu_ref[...].astype(BF16))
    act = (_silu(gate) * up).astype(BF16)
    res = _dot(act, wo_ref[...].astype(BF16))
    o_ref[...] += res[0:tm]
    accs_ref[...] += res[tm:tm + BF16_SUBLANES]

    @pl.when(j == pl.num_programs(1) - 1)
    def _():
        o_ref[...] = x_ref[...] + 0.5 * o_ref[...]
        os_ref[...] = xs_ref[...] + 0.5 * accs_ref[0:ns, :]


def _ffn(x, xs, norm_g, w_in, w_out, layer, half, gain_idx, tm=1024, tf=256):
    m = x.shape[0]
    ns = xs.shape[0]
    nj = D_FF // tf
    return pl.pallas_call(
        _ffn_kernel,
        grid=(m // tm, nj),
        in_specs=[
            pl.BlockSpec((tm, D_MODEL), lambda i, j: (i, 0), pipeline_mode=pl.Buffered(1)),
            pl.BlockSpec((ns, D_MODEL), lambda i, j: (0, 0)),
            pl.BlockSpec((None, None, 1, D_MODEL), lambda i, j: (layer, gain_idx, 0, 0)),
            pl.BlockSpec((None, None, D_MODEL, tf), lambda i, j: (layer, half, 0, j)),
            pl.BlockSpec((None, None, D_MODEL, tf), lambda i, j: (layer, half, 0, j + nj)),
            pl.BlockSpec((None, None, tf, D_MODEL), lambda i, j: (layer, half, j, 0)),
        ],
        out_specs=[
            pl.BlockSpec((tm, D_MODEL), lambda i, j: (i, 0)),
            pl.BlockSpec((ns, D_MODEL), lambda i, j: (0, 0)),
        ],
        out_shape=[
            jax.ShapeDtypeStruct((m, D_MODEL), F32),
            jax.ShapeDtypeStruct((ns, D_MODEL), F32),
        ],
        scratch_shapes=[
            pltpu.VMEM((tm + BF16_SUBLANES, D_MODEL), BF16),
            pltpu.VMEM((BF16_SUBLANES, D_MODEL), F32),
        ],
        compiler_params=_params("arbitrary", "arbitrary"),
        name="ffn",
    )(x, xs, norm_g, w_in, w_in, w_out)


def _head_norm(y, gain, scale):
    parts = []
    for h in range(DIL_HEADS):
        yh = y[:, h * DIL_HD:(h + 1) * DIL_HD]
        parts.append(yh * _rms_scale(yh) * gain * scale)
    return jnp.concatenate(parts, axis=-1)


def _norm_matmul_kernel(x_ref, g_ref, w_ref, hg_ref, o_ref, xn_ref, *, mode):
    j = pl.program_id(1)

    @pl.when(j == 0)
    def _():
        x = x_ref[...]
        xn_ref[...] = (x * _rms_scale(x) * g_ref[...]).astype(BF16)

    y = _dot(xn_ref[...], w_ref[...])
    if mode == "plain":
        o_ref[...] = y
    elif mode == "q":
        o_ref[...] = _head_norm(y, hg_ref[pl.ds(j, 1), :], DIL_HD ** -0.5)
    else:
        @pl.when(j % 2 == 0)
        def _():
            o_ref[...] = _head_norm(y, hg_ref[pl.ds(j // 2, 1), :], 1.0)

        @pl.when(j % 2 == 1)
        def _():
            o_ref[...] = y


def _norm_matmul(x, gain_arr, gain_spec, w_arr, w_spec, n, head_gain, mode, tm, tn):
    m = x.shape[0]
    return pl.pallas_call(
        functools.partial(_norm_matmul_kernel, mode=mode),
        grid=(m // tm, n // tn),
        in_specs=[
            pl.BlockSpec((tm, D_MODEL), lambda i, j: (i, 0)),
            gain_spec,
            w_spec,
            pl.BlockSpec(head_gain.shape, lambda i, j: (0, 0)),
        ],
        out_specs=pl.BlockSpec((tm, tn), lambda i, j: (i, j)),
        out_shape=jax.ShapeDtypeStruct((m, n), F32),
        scratch_shapes=[pltpu.VMEM((tm, D_MODEL), BF16)],
        compiler_params=_params("parallel", "arbitrary"),
        name="norm_matmul_" + mode,
    )(x, gain_arr, w_arr, head_gain)


def _proj_streams_kernel(x_ref, g_ref, w_ref, hg_ref, *refs, mode):
    n_out = N_GROUPS if mode == "q" else 2 * N_GROUPS
    outs = refs[:n_out]
    xn_ref, y_ref = refs[n_out:]
    j = pl.program_id(1)
    tm = x_ref.shape[0]

    @pl.when(j == 0)
    def _():
        x = x_ref[...]
        xn_ref[...] = (x * _rms_scale(x) * g_ref[...]).astype(BF16)

    y = _dot(xn_ref[...], w_ref[...])
    for jj in range(n_out):
        group = jj if mode == "q" else jj // 2
        normed = mode == "q" or jj % 2 == 0
        scale = DIL_HD ** -0.5 if mode == "q" else 1.0
        dil = DIL_GROUPS[group][1]
        out = outs[jj]

        @pl.when(j == jj)
        def _(group=group, normed=normed, scale=scale, dil=dil, out=out):
            val = _head_norm(y, hg_ref[group:group + 1, :], scale) if normed else y
            if dil == 1:
                out[0] = val.astype(out.dtype)
            else:
                for h in range(DIL_HEADS):
                    y_ref[h] = val[:, h * DIL_HD:(h + 1) * DIL_HD]
                for r in range(dil):
                    for h in range(DIL_HEADS):
                        out[r, :, h * DIL_HD:(h + 1) * DIL_HD] = y_ref[
                            h, pl.ds(r, tm // dil, stride=dil), :].astype(out.dtype)


def _proj_streams(x, gain_arr, gain_spec, w_arr, w_spec, head_gain, mode, batch, seq, tm, out_dtype):
    n_out = N_GROUPS if mode == "q" else 2 * N_GROUPS
    nb = seq // tm
    out_specs, out_shapes = [], []
    for jj in range(n_out):
        dil = DIL_GROUPS[jj if mode == "q" else jj // 2][1]
        out_specs.append(pl.BlockSpec((None, dil, tm // dil, DIL_DIM),
                                      lambda i, j: (i // nb, 0, i % nb, 0)))
        out_shapes.append(jax.ShapeDtypeStruct((batch, dil, seq // dil, DIL_DIM), out_dtype))
    return pl.pallas_call(
        functools.partial(_proj_streams_kernel, mode=mode),
        grid=(batch * nb, n_out),
        in_specs=[
            pl.BlockSpec((tm, D_MODEL), lambda i, j: (i, 0)),
            gain_spec,
            w_spec,
            pl.BlockSpec(head_gain.shape, lambda i, j: (0, 0)),
        ],
        out_specs=out_specs,
        out_shape=out_shapes,
        scratch_shapes=[pltpu.VMEM((tm, D_MODEL), BF16), pltpu.VMEM((DIL_HEADS, tm, DIL_HD), F32)],
        compiler_params=_params("parallel", "arbitrary"),
        name="proj_streams_" + mode,
    )(x, gain_arr, w_arr, head_gain)


def _matmul_res_kernel(a_ref, w_ref, r_ref, o_ref):
    o_ref[...] = r_ref[...] + _dot(a_ref[...].astype(BF16), w_ref[...])


def _matmul_res(a, w_arr, layer, res, tm, tn):
    m, k = a.shape
    n = res.shape[1]
    return pl.pallas_call(
        _matmul_res_kernel,
        grid=(m // tm, n // tn),
        in_specs=[
            pl.BlockSpec((tm, k), lambda i, j: (i, 0)),
            pl.BlockSpec((None, k, tn), lambda i, j: (layer, 0, j)),
            pl.BlockSpec((tm, tn), lambda i, j: (i, j)),
        ],
        out_specs=pl.BlockSpec((tm, tn), lambda i, j: (i, j)),
        out_shape=jax.ShapeDtypeStruct((m, n), F32),
        compiler_params=_params("parallel", "arbitrary"),
        name="matmul_res",
    )(a, w_arr, res)


def _hgrn_gates(z, lb):
    e = jnp.exp(-jnp.abs(z))
    inv = 1.0 / (1.0 + e)
    pos = z >= 0
    sig = jnp.where(pos, inv, e * inv)
    sig_neg = jnp.where(pos, e * inv, inv)
    k = (1.0 - lb) * sig_neg
    f = lb + (1.0 - lb) * sig
    return f, k, jnp.log(jnp.maximum(f, 1e-30))


def _hgrn_pairwise_chunk(q, k, v, cum):
    c = q.shape[0]
    n_sub = c // 8
    sub_row = lax.broadcasted_iota(jnp.int32, (8, 1), 0)
    acc = [jnp.zeros((8, HG_DK), F32) for _ in range(n_sub)]
    for s in range(c):
        cs = cum[s:s + 1, :]
        ks = k[s:s + 1, :]
        vs = v[s:s + 1, :]
        for tg in range(s // 8, n_sub):
            rows = slice(tg * 8, (tg + 1) * 8)
            p = q[rows, :] * ks * jnp.exp(cum[rows, :] - cs)
            a = jnp.sum(p, axis=-1, keepdims=True)
            if tg == s // 8:
                a = jnp.where(sub_row >= (s - tg * 8), a, 0.0)
            acc[tg] = acc[tg] + a * vs
    return jnp.concatenate(acc, axis=0)


def _hgrn_scan_kernel(q_ref, z_ref, v_ref, gt_ref, lb_ref, go_ref, o_ref, s_ref,
                      st_ref, oi_ref, k_ref, cum_ref):
    c = HG_CHUNK
    tile = HG_TILE
    n_chunks = tile // c
    h = pl.program_id(1)
    lb = lb_ref[pl.ds(h, 1), :]
    g_out = go_ref[...]
    row = lax.broadcasted_iota(jnp.int32, (tile, tile), 0)
    col = lax.broadcasted_iota(jnp.int32, (tile, tile), 1)
    shift = c.bit_length() - 1
    same_chunk = jnp.right_shift(row, shift) == jnp.right_shift(col, shift)
    tri = jnp.where(jnp.logical_and(same_chunk, col <= row), 1.0, 0.0).astype(BF16)
    crow = lax.broadcasted_iota(jnp.int32, (c, c), 0)
    ccol = lax.broadcasted_iota(jnp.int32, (c, c), 1)
    causal = ccol <= crow
    st_ref[...] = jnp.zeros_like(st_ref)

    def body(ti, carry):
        r0 = pl.multiple_of(ti * tile, tile)
        q = q_ref[pl.ds(r0, tile), :]
        v = v_ref[pl.ds(r0, tile), :]
        _, k, lf = _hgrn_gates(z_ref[pl.ds(r0, tile), :], lb)
        hi = lf.astype(BF16)
        rem = lf - hi.astype(F32)
        mid = rem.astype(BF16)
        lo = (rem - mid.astype(F32)).astype(BF16)
        cum = _dot(tri, jnp.concatenate([hi, mid, lo], axis=1))
        cum = cum[:, 0:HG_DK] + cum[:, HG_DK:2 * HG_DK] + cum[:, 2 * HG_DK:3 * HG_DK]
        chunks = [slice(ci * c, (ci + 1) * c) for ci in range(n_chunks)]
        lasts = [cum[sl.stop - 1:sl.stop, :] for sl in chunks]
        qdb = (q * jnp.exp(cum)).astype(BF16)
        vb = v.astype(BF16)
        khb = (k * jnp.exp(-cum)).astype(BF16)
        atts = [lax.dot_general(qdb[sl], khb[sl], _NT, preferred_element_type=F32) for sl in chunks]
        atts = [jnp.where(causal, att, 0.0).astype(BF16) for att in atts]
        intra = [_dot(att, vb[sl]) for att, sl in zip(atts, chunks)]
        kdb = jnp.concatenate([k[sl] * jnp.exp(last - cum[sl]) for sl, last in zip(chunks, lasts)],
                              axis=0).astype(BF16)
        incs = [lax.dot_general(vb[sl], kdb[sl], _TN, preferred_element_type=F32) for sl in chunks]
        sts = [st_ref[...]]
        for inc, last in zip(incs, lasts):
            sts.append(sts[-1] * jnp.exp(last) + inc)
        st_ref[...] = sts[-1]
        inter = [lax.dot_general(qdb[sl], st.astype(BF16), _NT, preferred_element_type=F32)
                 for sl, st in zip(chunks, sts)]
        oi_ref[...] = jnp.concatenate(intra, axis=0) + jnp.concatenate(inter, axis=0)

        @pl.when(jnp.min(cum) <= -HG_SAFE_DECAY)
        def _():
            k_ref[...] = k
            cum_ref[...] = cum
            oi_ref[...] = jnp.concatenate(inter, axis=0)

            def chunk(ci, carry2):
                c0 = pl.multiple_of(ci * c, c)
                oi_ref[pl.ds(c0, c), :] += _hgrn_pairwise_chunk(
                    q_ref[pl.ds(r0 + c0, c), :], k_ref[pl.ds(c0, c), :],
                    v_ref[pl.ds(r0 + c0, c), :], cum_ref[pl.ds(c0, c), :])
                return carry2

            lax.fori_loop(0, n_chunks, chunk, 0)

        o = oi_ref[...]
        o = o * _rms_scale(o) * g_out * _silu(gt_ref[pl.ds(r0, tile), :])
        o_ref[pl.ds(r0, tile), :] = o.astype(o_ref.dtype)
        return carry

    lax.fori_loop(0, q_ref.shape[0] // tile, body, 0)
    s_ref[...] = st_ref[...].T


def _hgrn_scan(proj, lb, g_out, batch, seq):
    nh = HG_HEADS
    col = lambda k: (lambda b, h: (b, k * nh + h))
    return pl.pallas_call(
        _hgrn_scan_kernel,
        grid=(batch, nh),
        in_specs=[
            pl.BlockSpec((seq, HG_DK), col(0)),
            pl.BlockSpec((seq, HG_DK), col(1)),
            pl.BlockSpec((seq, HG_DK), col(2)),
            pl.BlockSpec((seq, HG_DK), col(3)),
            pl.BlockSpec((nh, HG_DK), lambda b, h: (0, 0)),
            pl.BlockSpec((1, HG_DK), lambda b, h: (0, 0)),
        ],
        out_specs=[
            pl.BlockSpec((seq, HG_DK), lambda b, h: (b, h)),
            pl.BlockSpec((None, None, HG_DK, HG_DK), lambda b, h: (b, h, 0, 0)),
        ],
        out_shape=[
            jax.ShapeDtypeStruct((batch * seq, D_MODEL), BF16),
            jax.ShapeDtypeStruct((batch, nh, HG_DK, HG_DK), F32),
        ],
        scratch_shapes=[
            pltpu.VMEM((HG_DK, HG_DK), F32),
            pltpu.VMEM((HG_TILE, HG_DK), F32),
            pltpu.VMEM((HG_TILE, HG_DK), F32),
            pltpu.VMEM((HG_TILE, HG_DK), F32),
        ],
        compiler_params=_params("parallel", "parallel"),
        name="hgrn_scan",
    )(proj, proj, proj, proj, lb, g_out)


def _hgrn_step_kernel(x_ref, s0_ref, lb_ref, go_ref, o_ref, s_ref):
    nh = HG_HEADS
    x = x_ref[...]
    q = x[0:nh]
    f, k, _ = _hgrn_gates(x[nh:2 * nh], lb_ref[...])
    r = lax.broadcasted_iota(jnp.int32, (HG_DK, HG_DK), 0)
    c = lax.broadcasted_iota(jnp.int32, (HG_DK, HG_DK), 1)
    eye = (r == c).astype(F32)
    cols = lax.dot_general(eye, jnp.concatenate([q, f, k], axis=0), _NT,
                           preferred_element_type=F32, precision=lax.Precision.HIGHEST)
    g_out = go_ref[...]
    for h in range(nh):
        qc = cols[:, h:h + 1]
        fc = cols[:, nh + h:nh + h + 1]
        kc = cols[:, 2 * nh + h:2 * nh + h + 1]
        v = x[2 * nh + h:2 * nh + h + 1, :]
        s_new = fc * s0_ref[h] + kc * v
        s_ref[h] = s_new
        o = jnp.sum(s_new * qc, axis=0, keepdims=True)
        o = o * _rms_scale(o) * g_out * _silu(x[3 * nh + h:3 * nh + h + 1, :])
        o_ref[:, h * HG_DK:(h + 1) * HG_DK] = o


def _hgrn_step(proj, state, layer, lb, g_out):
    b = proj.shape[0]
    nh = HG_HEADS
    o, s = pl.pallas_call(
        _hgrn_step_kernel,
        grid=(b,),
        in_specs=[
            pl.BlockSpec((None, 4 * nh, HG_DK), lambda i: (i, 0, 0)),
            pl.BlockSpec((None, None, nh, HG_DK, HG_DK), lambda i: (layer, i, 0, 0, 0)),
            pl.BlockSpec((nh, HG_DK), lambda i: (0, 0)),
            pl.BlockSpec((1, HG_DK), lambda i: (0, 0)),
        ],
        out_specs=[
            pl.BlockSpec((None, 1, D_MODEL), lambda i: (i, 0, 0)),
            pl.BlockSpec((None, nh, HG_DK, HG_DK), lambda i: (i, 0, 0, 0)),
        ],
        out_shape=[
            jax.ShapeDtypeStruct((b, 1, D_MODEL), F32),
            jax.ShapeDtypeStruct((b, nh, HG_DK, HG_DK), F32),
        ],
        compiler_params=_params("parallel"),
        name="hgrn_step",
    )(proj.reshape(b, 4 * nh, HG_DK), state, lb, g_out)
    return o.reshape(b, D_MODEL), s


def _attn_prompt_kernel(q_ref, kc_ref, kp_ref, vc_ref, vp_ref, o_ref, lse_ref):
    ut = pl.program_id(2)
    qi = lax.broadcasted_iota(jnp.int32, (BAND, BAND), 0)
    kj = lax.broadcasted_iota(jnp.int32, (BAND, BAND), 1)
    steps_c = (qi - kj).astype(F32)
    steps_p = steps_c + float(BAND)
    valid_c = kj <= qi
    valid_p = jnp.logical_and(kj >= qi, ut > 0)
    lane = lax.broadcasted_iota(jnp.int32, (BAND, DIL_HD), 1)
    lse_all = jnp.zeros((BAND, DIL_HD), F32)
    for h in range(DIL_HEADS):
        sl = slice(h * DIL_HD, (h + 1) * DIL_HD)
        slope = 2.0 ** -(h + 1)
        q = q_ref[:, sl]
        sc = lax.dot_general(q, kc_ref[:, sl].astype(BF16), _NT, preferred_element_type=F32)
        sp = lax.dot_general(q, kp_ref[:, sl].astype(BF16), _NT, preferred_element_type=F32)
        sc = jnp.where(valid_c, sc - slope * steps_c, NEG)
        sp = jnp.where(valid_p, sp - slope * steps_p, NEG)
        m = jnp.maximum(jnp.max(sc, axis=-1, keepdims=True), jnp.max(sp, axis=-1, keepdims=True))
        pc = jnp.exp(sc - m)
        pp = jnp.exp(sp - m)
        l = jnp.sum(pc, axis=-1, keepdims=True) + jnp.sum(pp, axis=-1, keepdims=True)
        o = (_dot(pc.astype(BF16), vc_ref[:, sl].astype(BF16))
             + _dot(pp.astype(BF16), vp_ref[:, sl].astype(BF16)))
        o_ref[:, sl] = o / l
        lse_all = jnp.where(lane == h, m + jnp.log(l), lse_all)
    lse_ref[...] = lse_all


def _attn_prompt_group(q, k, v, gi):
    batch, dil, ls, _ = q.shape
    nt = ls // BAND
    tile = lambda idx: pl.BlockSpec((None, None, BAND, DIL_DIM), idx)
    cur = lambda b, r, u: (b, r, u, 0)
    prev = lambda b, r, u: (b, r, jnp.maximum(u - 1, 0), 0)
    return pl.pallas_call(
        _attn_prompt_kernel,
        grid=(batch, dil, nt),
        in_specs=[tile(cur), tile(cur), tile(prev), tile(cur), tile(prev)],
        out_specs=[
            pl.BlockSpec((None, None, BAND, DIL_DIM), cur),
            pl.BlockSpec((None, None, BAND, DIL_HD), cur),
        ],
        out_shape=[
            jax.ShapeDtypeStruct((batch, dil, ls, DIL_DIM), F32),
            jax.ShapeDtypeStruct((batch, dil, ls, DIL_HD), F32),
        ],
        compiler_params=_params("parallel", "parallel", "arbitrary"),
        name="attn_prompt_g%d" % gi,
    )(q, k, k, v, v)


def _merge_matmul_res_kernel(o0, o1, o2, l0, l1, l2, w_ref, r_ref, out_ref, a_ref, ot_ref, lt_ref):
    j = pl.program_id(1)
    tm = a_ref.shape[0]

    @pl.when(j == 0)
    def _():
        for gi, (o_ref, l_ref) in enumerate(((o1, l1), (o2, l2))):
            dil = DIL_GROUPS[gi + 1][1]
            for r in range(dil):
                for h in range(DIL_HEADS):
                    ot_ref[gi, h, pl.ds(r, tm // dil, stride=dil), :] = o_ref[
                        r, :, h * DIL_HD:(h + 1) * DIL_HD]
                lt_ref[gi, pl.ds(r, tm // dil, stride=dil), :] = l_ref[r]
        a0, a1, a2 = l0[0], lt_ref[0], lt_ref[1]
        m = jnp.maximum(jnp.maximum(a0, a1), a2)
        e0, e1, e2 = jnp.exp(a0 - m), jnp.exp(a1 - m), jnp.exp(a2 - m)
        inv = 1.0 / (e0 + e1 + e2)
        w0, w1, w2 = e0 * inv, e1 * inv, e2 * inv
        for h in range(DIL_HEADS):
            sl = slice(h * DIL_HD, (h + 1) * DIL_HD)
            a_ref[:, sl] = (o0[0, :, sl] * w0[:, h:h + 1] + ot_ref[0, h] * w1[:, h:h + 1]
                            + ot_ref[1, h] * w2[:, h:h + 1]).astype(BF16)

    out_ref[...] = r_ref[...] + _dot(a_ref[...], w_ref[...])


def _merge_matmul_res(os, lses, w_arr, layer, res, seq, tm, tn):
    m = res.shape[0]
    nb = seq // tm
    in_specs = []
    for width in (DIL_DIM, DIL_HD):
        for _, dil in DIL_GROUPS:
            in_specs.append(pl.BlockSpec((None, dil, tm // dil, width),
                                         lambda i, j: (i // nb, 0, i % nb, 0)))
    in_specs += [
        pl.BlockSpec((None, DIL_DIM, tn), lambda i, j: (layer, 0, j)),
        pl.BlockSpec((tm, tn), lambda i, j: (i, j)),
    ]
    return pl.pallas_call(
        _merge_matmul_res_kernel,
        grid=(m // tm, D_MODEL // tn),
        in_specs=in_specs,
        out_specs=pl.BlockSpec((tm, tn), lambda i, j: (i, j)),
        out_shape=jax.ShapeDtypeStruct((m, D_MODEL), F32),
        scratch_shapes=[
            pltpu.VMEM((tm, DIL_DIM), BF16),
            pltpu.VMEM((N_GROUPS - 1, DIL_HEADS, tm, DIL_HD), F32),
            pltpu.VMEM((N_GROUPS - 1, tm, DIL_HD), F32),
        ],
        compiler_params=_params("parallel", "arbitrary"),
        name="merge_matmul_res",
    )(*os, *lses, w_arr, res)


def _attn_sample_kernel(q_ref, kvn_ref, c0_ref, c1_ref, c2_ref, o_ref):
    caches = (c0_ref, c1_ref, c2_ref)
    steps = float(BAND) - lax.broadcasted_iota(jnp.int32, (BAND, 1, 1), 0).astype(F32)
    head = lax.broadcasted_iota(jnp.int32, (1, DIL_HEADS, 1), 1)
    slope = jnp.zeros((1, DIL_HEADS, 1), F32)
    for h in range(DIL_HEADS):
        slope = jnp.where(head == h, 2.0 ** -(h + 1), slope)
    bias = slope * steps
    outs, lses = [], []
    for g in range(N_GROUPS):
        c_ref = caches[g]
        q = q_ref[g]
        kn, vn = kvn_ref[g, 0], kvn_ref[g, 1]
        sb = jnp.sum(c_ref[:, 0] * q[None], axis=-1, keepdims=True) - bias
        sn = jnp.sum(kn * q, axis=-1, keepdims=True)
        m = jnp.maximum(jnp.max(sb, axis=0), sn)
        pb = jnp.exp(sb - m[None])
        pn = jnp.exp(sn - m)
        l = jnp.sum(pb, axis=0) + pn
        outs.append((jnp.sum(pb * c_ref[:, 1], axis=0) + pn * vn) / l)
        lses.append(m + jnp.log(l))
    m = jnp.maximum(jnp.maximum(lses[0], lses[1]), lses[2])
    es = [jnp.exp(a - m) for a in lses]
    inv = 1.0 / (es[0] + es[1] + es[2])
    o_ref[...] = outs[0] * (es[0] * inv) + outs[1] * (es[1] * inv) + outs[2] * (es[2] * inv)


def _attn_sample(q, kv_new, caches):
    b = q.shape[0]
    views, cache_specs = [], []
    for cache, (_, dil) in zip(caches, DIL_GROUPS):
        wb = cache.shape[1]
        views.append(cache.reshape(b, wb // dil, dil, 2, DIL_HEADS, DIL_HD))
        cache_specs.append(pl.BlockSpec((None, BAND, None, 2, DIL_HEADS, DIL_HD),
                                        lambda i: (i, 0, 0, 0, 0, 0)))
    o = pl.pallas_call(
        _attn_sample_kernel,
        grid=(b,),
        in_specs=[
            pl.BlockSpec((None, N_GROUPS, DIL_HEADS, DIL_HD), lambda i: (i, 0, 0, 0)),
            pl.BlockSpec((None, N_GROUPS, 2, DIL_HEADS, DIL_HD), lambda i: (i, 0, 0, 0, 0)),
        ] + cache_specs,
        out_specs=pl.BlockSpec((None, DIL_HEADS, DIL_HD), lambda i: (i, 0, 0)),
        out_shape=jax.ShapeDtypeStruct((b, DIL_HEADS, DIL_HD), F32),
        compiler_params=_params("parallel"),
        name="attn_sample",
    )(q.reshape(b, N_GROUPS, DIL_HEADS, DIL_HD),
      kv_new.reshape(b, N_GROUPS, 2, DIL_HEADS, DIL_HD), *views)
    return o.reshape(b, DIL_DIM)


def _trunk(x, xs, batch, seq, hg_state, caches, wts):
    (norm_g, ffn_w_in, ffn_w_out, hg_w_in, hg_w_out, lbs, hg_out_g, kv_norm_g, w_kv,
     k_norm_g, dil_w_q, dil_q_norm_g, dil_w_o) = wts
    ns = xs.shape[0]
    tm, tn = 512, 1024
    states_p, states_s = [], []
    kv_p = kv_s = None
    gain = lambda layer, idx: pl.BlockSpec((None, None, 1, D_MODEL), lambda i, j: (layer, idx, 0, 0))
    for layer in range(DEPTH):
        x, xs = _ffn(x, xs, norm_g, ffn_w_in, ffn_w_out, layer, 0, 0)
        if layer < N_A_LAYERS:
            w_spec = pl.BlockSpec((None, D_MODEL, tn), lambda i, j, layer=layer: (layer, 0, j))
            lb = lbs[layer].reshape(HG_HEADS, HG_DK)
            g_out = hg_out_g[layer].reshape(1, HG_DK)
            proj = _norm_matmul(x, norm_g, gain(layer, 1), hg_w_in, w_spec,
                                4 * D_MODEL, k_norm_g, "plain", tm, tn)
            o, s = _hgrn_scan(proj, lb, g_out, batch, seq)
            states_p.append(s)
            x = _matmul_res(o, hg_w_out, layer, x, tm, tn)
            proj = _norm_matmul(xs, norm_g, gain(layer, 1), hg_w_in, w_spec,
                                4 * D_MODEL, k_norm_g, "plain", ns, tn)
            o, s = _hgrn_step(proj, hg_state, layer, lb, g_out)
            states_s.append(s)
            xs = _matmul_res(o, hg_w_out, layer, xs, ns, tn)
        else:
            bi = layer - N_A_LAYERS
            w_spec = pl.BlockSpec((None, D_MODEL, DIL_DIM), lambda i, j, bi=bi: (bi, 0, j))
            qs = _proj_streams(x, norm_g, gain(layer, 1), dil_w_q, w_spec, dil_q_norm_g[bi],
                               "q", batch, seq, tm, BF16)
            parts = [_attn_prompt_group(qs[gi], kv_p[2 * gi], kv_p[2 * gi + 1], gi)
                     for gi in range(N_GROUPS)]
            x = _merge_matmul_res([p[0] for p in parts], [p[1] for p in parts],
                                  dil_w_o, bi, x, seq, tm, tn)
            q = _norm_matmul(xs, norm_g, gain(layer, 1), dil_w_q, w_spec,
                             N_GROUPS * DIL_DIM, dil_q_norm_g[bi], "q", ns, DIL_DIM)
            xs = _matmul_res(_attn_sample(q, kv_s, caches), dil_w_o, bi, xs, ns, tn)
        x, xs = _ffn(x, xs, norm_g, ffn_w_in, ffn_w_out, layer, 1, 2)
        if layer == N_A_LAYERS - 1:
            kv_gain = kv_norm_g.reshape(1, D_MODEL)
            kv_gain_spec = pl.BlockSpec((1, D_MODEL), lambda i, j: (0, 0))
            w_spec = pl.BlockSpec((D_MODEL, DIL_DIM), lambda i, j: (0, j))
            kv_p = _proj_streams(x, kv_gain, kv_gain_spec, w_kv, w_spec, k_norm_g,
                                 "kv", batch, seq, tm, F32)
            kv_s = _norm_matmul(xs, kv_gain, kv_gain_spec, w_kv, w_spec,
                                2 * N_GROUPS * DIL_DIM, k_norm_g, "kv", ns, DIL_DIM)
    return x, xs, jnp.stack(states_p, axis=0), jnp.stack(states_s, axis=0), kv_p, kv_s


def kernel(x_prompt, x_sample, state_hgrn, cache_kv_w128, cache_kv_w512, cache_kv_w2048,
           norm_g, ffn_w_in, ffn_w_out, hg_w_in, hg_w_out, hg_lb_logits, hg_out_g,
           kv_norm_g, w_kv, k_norm_g, dil_w_q, dil_q_norm_g, dil_w_o):
    batch, seq, _ = x_prompt.shape
    dec_batch = x_sample.shape[0]
    p = jax.nn.softmax(hg_lb_logits.astype(F32), axis=0)
    lbs = jnp.cumsum(p, axis=0) - p[0]
    wts = (norm_g.reshape(DEPTH, 3, 1, D_MODEL), ffn_w_in, ffn_w_out,
           hg_w_in.astype(BF16), hg_w_out.astype(BF16), lbs, hg_out_g, kv_norm_g,
           w_kv.astype(BF16), k_norm_g, dil_w_q.astype(BF16), dil_q_norm_g, dil_w_o.astype(BF16))

    y_p, y_s, hg_p, hg_s, kv_p, kv_s = _trunk(
        x_prompt.reshape(batch * seq, D_MODEL), x_sample.reshape(dec_batch, D_MODEL), batch, seq,
        state_hgrn, (cache_kv_w128, cache_kv_w512, cache_kv_w2048), wts)

    kv_s = kv_s.reshape(dec_batch, 1, N_GROUPS, 2, DIL_HEADS, DIL_HD)
    outs = [y_p.reshape(batch, seq, D_MODEL), y_s.reshape(dec_batch, 1, D_MODEL), hg_p, hg_s]
    for gi, (win, dil) in enumerate(DIL_GROUPS):
        rows = min(win, seq)
        tail = [jnp.swapaxes(a[:, :, (seq - rows) // dil:], 1, 2).reshape(
            batch, rows, DIL_HEADS, DIL_HD) for a in (kv_p[2 * gi], kv_p[2 * gi + 1])]
        outs.append(jnp.stack(tail, axis=2))
        outs.append(kv_s[:, :, gi])
    return tuple(outs)
```

```python
import functools

import jax
import jax.numpy as jnp
from jax import lax
from jax.experimental import pallas as pl
from jax.experimental.pallas import tpu as pltpu

D_MODEL = 2048
D_FF = 5632
DEPTH = 4
N_A_LAYERS = 2
HG_HEADS = 16
HG_DK = 128
DIL_GROUPS = ((128, 1), (512, 4), (2048, 16))
N_GROUPS = 3
DIL_HEADS = 8
DIL_HD = 128
DIL_DIM = DIL_HEADS * DIL_HD
BAND = 128
EPS = 1e-6
NEG = -1e30
F32 = jnp.float32
BF16 = jnp.bfloat16

VMEM_LIMIT_BYTES = 56 * 1024 * 1024
BF16_SUBLANES = 16
HG_CHUNK = 32
HG_TILE = 512
HG_SAFE_DECAY = 80.0

_NT = (((1,), (1,)), ((), ()))
_TN = (((0,), (0,)), ((), ()))


def _params(*semantics):
    return pltpu.CompilerParams(dimension_semantics=semantics, vmem_limit_bytes=VMEM_LIMIT_BYTES)


def _rms_scale(x):
    return lax.rsqrt(jnp.mean(x * x, axis=-1, keepdims=True) + EPS)


def _silu(x):
    return x / (1.0 + jnp.exp(-x))


def _dot(a, b):
    return jnp.dot(a, b, preferred_element_type=F32)


def _ffn_kernel(x_ref, xs_ref, g_ref, wg_ref, wu_ref, wo_ref, o_ref, os_ref, xn_ref, accs_ref):
    j = pl.program_id(1)
    tm = x_ref.shape[0]
    ns = xs_ref.shape[0]

    @pl.when(j == 0)
    def _():
        g = g_ref[...]
        x = x_ref[...]
        xn_ref[0:tm, :] = (x * _rms_scale(x) * g).astype(BF16)
        xs = xs_ref[...]
        xsn = jnp.concatenate(
            [xs * _rms_scale(xs) * g, jnp.zeros((BF16_SUBLANES - ns, D_MODEL), F32)], axis=0)
        xn_ref[tm:tm + BF16_SUBLANES, :] = xsn.astype(BF16)
        o_ref[...] = jnp.zeros_like(o_ref)
        accs_ref[...] = jnp.zeros_like(accs_ref)

    xn = xn_ref[...]
    gate = _dot(xn, wg_ref[...].astype(BF16))
    up = _dot(xn, wu_ref[...].astype(BF16))
    act = (_silu(gate) * up).astype(BF16)
    res = _dot(act, wo_ref[...].astype(BF16))
    o_ref[...] += res[0:tm]
    accs_ref[...] += res[tm:tm + BF16_SUBLANES]

    @pl.when(j == pl.num_programs(1) - 1)
    def _():
        o_ref[...] = x_ref[...] + 0.5 * o_ref[...]
        os_ref[...] = xs_ref[...] + 0.5 * accs_ref[0:ns, :]


def _ffn(x, xs, norm_g, w_in, w_out, layer, half, gain_idx, tm=1024, tf=256):
    m = x.shape[0]
    ns = xs.shape[0]
    nj = D_FF // tf
    return pl.pallas_call(
        _ffn_kernel,
        grid=(m // tm, nj),
        in_specs=[
            pl.BlockSpec((tm, D_MODEL), lambda i, j: (i, 0)),
            pl.BlockSpec((ns, D_MODEL), lambda i, j: (0, 0)),
            pl.BlockSpec((None, None, 1, D_MODEL), lambda i, j: (layer, gain_idx, 0, 0)),
            pl.BlockSpec((None, None, D_MODEL, tf), lambda i, j: (layer, half, 0, j)),
            pl.BlockSpec((None, None, D_MODEL, tf), lambda i, j: (layer, half, 0, j + nj)),
            pl.BlockSpec((None, None, tf, D_MODEL), lambda i, j: (layer, half, j, 0)),
        ],
        out_specs=[
            pl.BlockSpec((tm, D_MODEL), lambda i, j: (i, 0)),
            pl.BlockSpec((ns, D_MODEL), lambda i, j: (0, 0)),
        ],
        out_shape=[
            jax.ShapeDtypeStruct((m, D_MODEL), F32),
            jax.ShapeDtypeStruct((ns, D_MODEL), F32),
        ],
        scratch_shapes=[
            pltpu.VMEM((tm + BF16_SUBLANES, D_MODEL), BF16),
            pltpu.VMEM((BF16_SUBLANES, D_MODEL), F32),
        ],
        compiler_params=_params("arbitrary", "arbitrary"),
        name="ffn",
    )(x, xs, norm_g, w_in, w_in, w_out)


def _head_norm(y, gain, scale):
    parts = []
    for h in range(DIL_HEADS):
        yh = y[:, h * DIL_HD:(h + 1) * DIL_HD]
        parts.append(yh * _rms_scale(yh) * gain * scale)
    return jnp.concatenate(parts, axis=-1)


def _norm_matmul_kernel(x_ref, g_ref, w_ref, hg_ref, o_ref, xn_ref, *, mode):
    j = pl.program_id(1)

    @pl.when(j == 0)
    def _():
        x = x_ref[...]
        xn_ref[...] = (x * _rms_scale(x) * g_ref[...]).astype(BF16)

    y = _dot(xn_ref[...], w_ref[...])
    if mode == "plain":
        o_ref[...] = y
    elif mode == "q":
        o_ref[...] = _head_norm(y, hg_ref[pl.ds(j, 1), :], DIL_HD ** -0.5)
    else:
        @pl.when(j % 2 == 0)
        def _():
            o_ref[...] = _head_norm(y, hg_ref[pl.ds(j // 2, 1), :], 1.0)

        @pl.when(j % 2 == 1)
        def _():
            o_ref[...] = y


def _norm_matmul(x, gain_arr, gain_spec, w_arr, w_spec, n, head_gain, mode, tm, tn):
    m = x.shape[0]
    return pl.pallas_call(
        functools.partial(_norm_matmul_kernel, mode=mode),
        grid=(m // tm, n // tn),
        in_specs=[
            pl.BlockSpec((tm, D_MODEL), lambda i, j: (i, 0)),
            gain_spec,
            w_spec,
            pl.BlockSpec(head_gain.shape, lambda i, j: (0, 0)),
        ],
        out_specs=pl.BlockSpec((tm, tn), lambda i, j: (i, j)),
        out_shape=jax.ShapeDtypeStruct((m, n), F32),
        scratch_shapes=[pltpu.VMEM((tm, D_MODEL), BF16)],
        compiler_params=_params("parallel", "arbitrary"),
        name="norm_matmul_" + mode,
    )(x, gain_arr, w_arr, head_gain)


def _kv_streams_kernel(x_ref, g_ref, w_ref, hg_ref, *refs):
    outs = refs[:2 * N_GROUPS]
    xn_ref, y_ref = refs[2 * N_GROUPS:]
    j = pl.program_id(1)
    tm = x_ref.shape[0]

    @pl.when(j == 0)
    def _():
        x = x_ref[...]
        xn_ref[...] = (x * _rms_scale(x) * g_ref[...]).astype(BF16)

    y = _dot(xn_ref[...], w_ref[...])
    for jj in range(2 * N_GROUPS):
        group = jj // 2
        is_key = jj % 2 == 0
        dil = DIL_GROUPS[group][1]
        out = outs[jj]

        @pl.when(j == jj)
        def _(group=group, is_key=is_key, dil=dil, out=out):
            val = _head_norm(y, hg_ref[group:group + 1, :], 1.0) if is_key else y
            if dil == 1:
                out[0] = val
            else:
                for h in range(DIL_HEADS):
                    y_ref[h] = val[:, h * DIL_HD:(h + 1) * DIL_HD]
                for r in range(dil):
                    for h in range(DIL_HEADS):
                        out[r, :, h * DIL_HD:(h + 1) * DIL_HD] = y_ref[
                            h, pl.ds(r, tm // dil, stride=dil), :]


def _kv_streams(x, gain_arr, gain_spec, w_arr, w_spec, head_gain, batch, seq, tm):
    nb = seq // tm
    out_specs, out_shapes = [], []
    for jj in range(2 * N_GROUPS):
        dil = DIL_GROUPS[jj // 2][1]
        out_specs.append(pl.BlockSpec((None, dil, tm // dil, DIL_DIM),
                                      lambda i, j: (i // nb, 0, i % nb, 0)))
        out_shapes.append(jax.ShapeDtypeStruct((batch, dil, seq // dil, DIL_DIM), F32))
    return pl.pallas_call(
        _kv_streams_kernel,
        grid=(batch * nb, 2 * N_GROUPS),
        in_specs=[
            pl.BlockSpec((tm, D_MODEL), lambda i, j: (i, 0)),
            gain_spec,
            w_spec,
            pl.BlockSpec(head_gain.shape, lambda i, j: (0, 0)),
        ],
        out_specs=out_specs,
        out_shape=out_shapes,
        scratch_shapes=[pltpu.VMEM((tm, D_MODEL), BF16), pltpu.VMEM((DIL_HEADS, tm, DIL_HD), F32)],
        compiler_params=_params("parallel", "arbitrary"),
        name="kv_streams",
    )(x, gain_arr, w_arr, head_gain)


def _q_streams_kernel(x_ref, g_ref, w_ref, hg_ref, o0_ref, o1_ref, o2_ref, y_ref):
    tm = x_ref.shape[0]
    x = x_ref[...]
    xn = (x * _rms_scale(x) * g_ref[...]).astype(BF16)
    for group, out in enumerate((o0_ref, o1_ref, o2_ref)):
        dil = DIL_GROUPS[group][1]
        y = _dot(xn, w_ref[:, group * DIL_DIM:(group + 1) * DIL_DIM])
        val = _head_norm(y, hg_ref[group:group + 1, :], DIL_HD ** -0.5)
        if dil == 1:
            out[0] = val.astype(out.dtype)
        else:
            for h in range(DIL_HEADS):
                y_ref[group - 1, h] = val[:, h * DIL_HD:(h + 1) * DIL_HD]
            for r in range(dil):
                for h in range(DIL_HEADS):
                    out[r, :, h * DIL_HD:(h + 1) * DIL_HD] = y_ref[
                        group - 1, h, pl.ds(r, tm // dil, stride=dil), :].astype(out.dtype)


def _q_streams(x, gain_arr, gain_spec, w_arr, bi, head_gain, batch, seq, tm):
    nb = seq // tm
    out_specs, out_shapes = [], []
    for _, dil in DIL_GROUPS:
        out_specs.append(pl.BlockSpec((None, dil, tm // dil, DIL_DIM),
                                      lambda i: (i // nb, 0, i % nb, 0)))
        out_shapes.append(jax.ShapeDtypeStruct((batch, dil, seq // dil, DIL_DIM), BF16))
    return pl.pallas_call(
        _q_streams_kernel,
        grid=(batch * nb,),
        in_specs=[
            pl.BlockSpec((tm, D_MODEL), lambda i: (i, 0)),
            gain_spec,
            pl.BlockSpec((None, D_MODEL, N_GROUPS * DIL_DIM), lambda i: (bi, 0, 0),
                         pipeline_mode=pl.Buffered(1)),
            pl.BlockSpec(head_gain.shape, lambda i: (0, 0)),
        ],
        out_specs=out_specs,
        out_shape=out_shapes,
        scratch_shapes=[pltpu.VMEM((N_GROUPS - 1, DIL_HEADS, tm, DIL_HD), F32)],
        compiler_params=_params("parallel"),
        name="q_streams",
    )(x, gain_arr, w_arr, head_gain)


def _matmul_res_kernel(a_ref, w_ref, r_ref, o_ref):
    o_ref[...] = r_ref[...] + _dot(a_ref[...].astype(BF16), w_ref[...])


def _matmul_res(a, w_arr, layer, res, tm):
    m, k = a.shape
    n = res.shape[1]
    return pl.pallas_call(
        _matmul_res_kernel,
        grid=(m // tm,),
        in_specs=[
            pl.BlockSpec((tm, k), lambda i: (i, 0)),
            pl.BlockSpec((None, k, n), lambda i: (layer, 0, 0), pipeline_mode=pl.Buffered(1)),
            pl.BlockSpec((tm, n), lambda i: (i, 0)),
        ],
        out_specs=pl.BlockSpec((tm, n), lambda i: (i, 0)),
        out_shape=jax.ShapeDtypeStruct((m, n), F32),
        compiler_params=_params("parallel"),
        name="matmul_res",
    )(a, w_arr, res)


def _hgrn_gates(z, lb):
    e = jnp.exp(-jnp.abs(z))
    inv = 1.0 / (1.0 + e)
    pos = z >= 0
    sig = jnp.where(pos, inv, e * inv)
    sig_neg = jnp.where(pos, e * inv, inv)
    k = (1.0 - lb) * sig_neg
    f = lb + (1.0 - lb) * sig
    return f, k, jnp.log(jnp.maximum(f, 1e-30))


def _hgrn_pairwise_chunk(q, k, v, cum):
    c = q.shape[0]
    n_sub = c // 8
    sub_row = lax.broadcasted_iota(jnp.int32, (8, 1), 0)
    acc = [jnp.zeros((8, HG_DK), F32) for _ in range(n_sub)]
    for s in range(c):
        cs = cum[s:s + 1, :]
        ks = k[s:s + 1, :]
        vs = v[s:s + 1, :]
        for tg in range(s // 8, n_sub):
            rows = slice(tg * 8, (tg + 1) * 8)
            p = q[rows, :] * ks * jnp.exp(cum[rows, :] - cs)
            a = jnp.sum(p, axis=-1, keepdims=True)
            if tg == s // 8:
                a = jnp.where(sub_row >= (s - tg * 8), a, 0.0)
            acc[tg] = acc[tg] + a * vs
    return jnp.concatenate(acc, axis=0)


def _hgrn_scan_kernel(q_ref, z_ref, v_ref, gt_ref, lb_ref, go_ref, o_ref, s_ref,
                      st_ref, oi_ref, k_ref, cum_ref):
    c = HG_CHUNK
    tile = HG_TILE
    n_chunks = tile // c
    h = pl.program_id(1)
    lb = lb_ref[pl.ds(h, 1), :]
    g_out = go_ref[...]
    row = lax.broadcasted_iota(jnp.int32, (tile, tile), 0)
    col = lax.broadcasted_iota(jnp.int32, (tile, tile), 1)
    shift = c.bit_length() - 1
    same_chunk = jnp.right_shift(row, shift) == jnp.right_shift(col, shift)
    tri = jnp.where(jnp.logical_and(same_chunk, col <= row), 1.0, 0.0).astype(BF16)
    crow = lax.broadcasted_iota(jnp.int32, (c, c), 0)
    ccol = lax.broadcasted_iota(jnp.int32, (c, c), 1)
    causal = ccol <= crow
    st_ref[...] = jnp.zeros_like(st_ref)

    def body(ti, carry):
        r0 = pl.multiple_of(ti * tile, tile)
        q = q_ref[pl.ds(r0, tile), :]
        v = v_ref[pl.ds(r0, tile), :]
        _, k, lf = _hgrn_gates(z_ref[pl.ds(r0, tile), :], lb)
        hi = lf.astype(BF16)
        rem = lf - hi.astype(F32)
        mid = rem.astype(BF16)
        lo = (rem - mid.astype(F32)).astype(BF16)
        cum = _dot(tri, jnp.concatenate([hi, mid, lo], axis=1))
        cum = cum[:, 0:HG_DK] + cum[:, HG_DK:2 * HG_DK] + cum[:, 2 * HG_DK:3 * HG_DK]
        chunks = [slice(ci * c, (ci + 1) * c) for ci in range(n_chunks)]
        lasts = [cum[sl.stop - 1:sl.stop, :] for sl in chunks]
        qdb = (q * jnp.exp(cum)).astype(BF16)
        vb = v.astype(BF16)
        khb = (k * jnp.exp(-cum)).astype(BF16)
        atts = [lax.dot_general(qdb[sl], khb[sl], _NT, preferred_element_type=F32) for sl in chunks]
        atts = [jnp.where(causal, att, 0.0).astype(BF16) for att in atts]
        intra = [_dot(att, vb[sl]) for att, sl in zip(atts, chunks)]
        kdb = jnp.concatenate([k[sl] * jnp.exp(last - cum[sl]) for sl, last in zip(chunks, lasts)],
                              axis=0).astype(BF16)
        incs = [lax.dot_general(vb[sl], kdb[sl], _TN, preferred_element_type=F32) for sl in chunks]
        sts = [st_ref[...]]
        for inc, last in zip(incs, lasts):
            sts.append(sts[-1] * jnp.exp(last) + inc)
        st_ref[...] = sts[-1]
        inter = [lax.dot_general(qdb[sl], st.astype(BF16), _NT, preferred_element_type=F32)
                 for sl, st in zip(chunks, sts)]
        oi_ref[...] = jnp.concatenate(intra, axis=0) + jnp.concatenate(inter, axis=0)

        @pl.when(jnp.min(cum) <= -HG_SAFE_DECAY)
        def _():
            k_ref[...] = k
            cum_ref[...] = cum
            oi_ref[...] = jnp.concatenate(inter, axis=0)

            def chunk(ci, carry2):
                c0 = pl.multiple_of(ci * c, c)
                oi_ref[pl.ds(c0, c), :] += _hgrn_pairwise_chunk(
                    q_ref[pl.ds(r0 + c0, c), :], k_ref[pl.ds(c0, c), :],
                    v_ref[pl.ds(r0 + c0, c), :], cum_ref[pl.ds(c0, c), :])
                return carry2

            lax.fori_loop(0, n_chunks, chunk, 0)

        o = oi_ref[...]
        o = o * _rms_scale(o) * g_out * _silu(gt_ref[pl.ds(r0, tile), :])
        o_ref[pl.ds(r0, tile), :] = o.astype(o_ref.dtype)
        return carry

    lax.fori_loop(0, q_ref.shape[0] // tile, body, 0)
    s_ref[...] = st_ref[...].T


def _hgrn_scan(proj, lb, g_out, batch, seq):
    nh = HG_HEADS
    col = lambda k: (lambda b, h: (b, k * nh + h))
    return pl.pallas_call(
        _hgrn_scan_kernel,
        grid=(batch, nh),
        in_specs=[
            pl.BlockSpec((seq, HG_DK), col(0)),
            pl.BlockSpec((seq, HG_DK), col(1)),
            pl.BlockSpec((seq, HG_DK), col(2)),
            pl.BlockSpec((seq, HG_DK), col(3)),
            pl.BlockSpec((nh, HG_DK), lambda b, h: (0, 0)),
            pl.BlockSpec((1, HG_DK), lambda b, h: (0, 0)),
        ],
        out_specs=[
            pl.BlockSpec((seq, HG_DK), lambda b, h: (b, h)),
            pl.BlockSpec((None, None, HG_DK, HG_DK), lambda b, h: (b, h, 0, 0)),
        ],
        out_shape=[
            jax.ShapeDtypeStruct((batch * seq, D_MODEL), BF16),
            jax.ShapeDtypeStruct((batch, nh, HG_DK, HG_DK), F32),
        ],
        scratch_shapes=[
            pltpu.VMEM((HG_DK, HG_DK), F32),
            pltpu.VMEM((HG_TILE, HG_DK), F32),
            pltpu.VMEM((HG_TILE, HG_DK), F32),
            pltpu.VMEM((HG_TILE, HG_DK), F32),
        ],
        compiler_params=_params("parallel", "parallel"),
        name="hgrn_scan",
    )(proj, proj, proj, proj, lb, g_out)


def _hgrn_step_kernel(x_ref, s0_ref, lb_ref, go_ref, o_ref, s_ref):
    nh = HG_HEADS
    x = x_ref[...]
    q = x[0:nh]
    f, k, _ = _hgrn_gates(x[nh:2 * nh], lb_ref[...])
    r = lax.broadcasted_iota(jnp.int32, (HG_DK, HG_DK), 0)
    c = lax.broadcasted_iota(jnp.int32, (HG_DK, HG_DK), 1)
    eye = (r == c).astype(F32)
    cols = lax.dot_general(eye, jnp.concatenate([q, f, k], axis=0), _NT,
                           preferred_element_type=F32, precision=lax.Precision.HIGHEST)
    g_out = go_ref[...]
    for h in range(nh):
        qc = cols[:, h:h + 1]
        fc = cols[:, nh + h:nh + h + 1]
        kc = cols[:, 2 * nh + h:2 * nh + h + 1]
        v = x[2 * nh + h:2 * nh + h + 1, :]
        s_new = fc * s0_ref[h] + kc * v
        s_ref[h] = s_new
        o = jnp.sum(s_new * qc, axis=0, keepdims=True)
        o = o * _rms_scale(o) * g_out * _silu(x[3 * nh + h:3 * nh + h + 1, :])
        o_ref[:, h * HG_DK:(h + 1) * HG_DK] = o


def _hgrn_step(proj, state, layer, lb, g_out):
    b = proj.shape[0]
    nh = HG_HEADS
    o, s = pl.pallas_call(
        _hgrn_step_kernel,
        grid=(b,),
        in_specs=[
            pl.BlockSpec((None, 4 * nh, HG_DK), lambda i: (i, 0, 0)),
            pl.BlockSpec((None, None, nh, HG_DK, HG_DK), lambda i: (layer, i, 0, 0, 0)),
            pl.BlockSpec((nh, HG_DK), lambda i: (0, 0)),
            pl.BlockSpec((1, HG_DK), lambda i: (0, 0)),
        ],
        out_specs=[
            pl.BlockSpec((None, 1, D_MODEL), lambda i: (i, 0, 0)),
            pl.BlockSpec((None, nh, HG_DK, HG_DK), lambda i: (i, 0, 0, 0)),
        ],
        out_shape=[
            jax.ShapeDtypeStruct((b, 1, D_MODEL), F32),
            jax.ShapeDtypeStruct((b, nh, HG_DK, HG_DK), F32),
        ],
        compiler_params=_params("parallel"),
        name="hgrn_step",
    )(proj.reshape(b, 4 * nh, HG_DK), state, lb, g_out)
    return o.reshape(b, D_MODEL), s


def _attn_prompt_kernel(q_ref, k_ref, v_ref, o_ref, lse_ref, kp_ref, vp_ref):
    ut = pl.program_id(2)

    @pl.when(ut == 0)
    def _():
        kp_ref[...] = jnp.zeros_like(kp_ref)
        vp_ref[...] = jnp.zeros_like(vp_ref)

    qi = lax.broadcasted_iota(jnp.int32, (BAND, BAND), 0)
    kj = lax.broadcasted_iota(jnp.int32, (BAND, BAND), 1)
    steps_c = (qi - kj).astype(F32)
    steps_p = steps_c + float(BAND)
    valid_c = kj <= qi
    valid_p = jnp.logical_and(kj >= qi, ut > 0)
    lane = lax.broadcasted_iota(jnp.int32, (BAND, DIL_HD), 1)
    heads = [slice(h * DIL_HD, (h + 1) * DIL_HD) for h in range(DIL_HEADS)]
    q = q_ref[...]
    kc, vc = k_ref[...].astype(BF16), v_ref[...].astype(BF16)
    kp, vp = kp_ref[...], vp_ref[...]
    sc = [lax.dot_general(q[:, sl], kc[:, sl], _NT, preferred_element_type=F32) for sl in heads]
    sp = [lax.dot_general(q[:, sl], kp[:, sl], _NT, preferred_element_type=F32) for sl in heads]
    pcs, pps, ms = [], [], []
    for h in range(DIL_HEADS):
        slope = 2.0 ** -(h + 1)
        c = jnp.where(valid_c, sc[h] - slope * steps_c, NEG)
        p = jnp.where(valid_p, sp[h] - slope * steps_p, NEG)
        m = jnp.max(jnp.maximum(c, p), axis=-1, keepdims=True)
        pcs.append(jnp.exp(c - m).astype(BF16))
        pps.append(jnp.exp(p - m).astype(BF16))
        ms.append(m)
    ones = jnp.ones((BAND, DIL_HD), BF16)
    ext = [_dot(pcs[h], jnp.concatenate([vc[:, sl], ones], axis=1))
           + _dot(pps[h], jnp.concatenate([vp[:, sl], ones], axis=1))
           for h, sl in enumerate(heads)]
    lse_all = jnp.zeros((BAND, DIL_HD), F32)
    for h, sl in enumerate(heads):
        l = ext[h][:, DIL_HD:]
        o_ref[:, sl] = ext[h][:, :DIL_HD] / l
        lse_all = jnp.where(lane == h, ms[h] + jnp.log(l), lse_all)
    lse_ref[...] = lse_all
    kp_ref[...] = kc
    vp_ref[...] = vc


def _attn_prompt_group(q, k, v, gi):
    batch, dil, ls, _ = q.shape
    tile = pl.BlockSpec((None, None, BAND, DIL_DIM), lambda b, r, u: (b, r, u, 0))
    return pl.pallas_call(
        _attn_prompt_kernel,
        grid=(batch, dil, ls // BAND),
        in_specs=[tile, tile, tile],
        out_specs=[tile, pl.BlockSpec((None, None, BAND, DIL_HD), lambda b, r, u: (b, r, u, 0))],
        out_shape=[
            jax.ShapeDtypeStruct((batch, dil, ls, DIL_DIM), F32),
            jax.ShapeDtypeStruct((batch, dil, ls, DIL_HD), F32),
        ],
        scratch_shapes=[pltpu.VMEM((BAND, DIL_DIM), BF16), pltpu.VMEM((BAND, DIL_DIM), BF16)],
        compiler_params=_params("parallel", "parallel", "arbitrary"),
        name="attn_prompt_g%d" % gi,
    )(q, k, v)


def _merge_matmul_res_kernel(o0, o1, o2, l0, l1, l2, w_ref, r_ref, out_ref, ot_ref, lt_ref):
    tm = out_ref.shape[0]
    for gi, (o_ref, l_ref) in enumerate(((o1, l1), (o2, l2))):
        dil = DIL_GROUPS[gi + 1][1]
        for r in range(dil):
            for h in range(DIL_HEADS):
                ot_ref[gi, h, pl.ds(r, tm // dil, stride=dil), :] = o_ref[
                    r, :, h * DIL_HD:(h + 1) * DIL_HD]
            lt_ref[gi, pl.ds(r, tm // dil, stride=dil), :] = l_ref[r]
    a0, a1, a2 = l0[0], lt_ref[0], lt_ref[1]
    m = jnp.maximum(jnp.maximum(a0, a1), a2)
    e0, e1, e2 = jnp.exp(a0 - m), jnp.exp(a1 - m), jnp.exp(a2 - m)
    inv = 1.0 / (e0 + e1 + e2)
    w0, w1, w2 = e0 * inv, e1 * inv, e2 * inv
    parts = []
    for h in range(DIL_HEADS):
        sl = slice(h * DIL_HD, (h + 1) * DIL_HD)
        parts.append((o0[0, :, sl] * w0[:, h:h + 1] + ot_ref[0, h] * w1[:, h:h + 1]
                      + ot_ref[1, h] * w2[:, h:h + 1]).astype(BF16))
    out_ref[...] = r_ref[...] + _dot(jnp.concatenate(parts, axis=1), w_ref[...])


def _merge_matmul_res(os, lses, w_arr, layer, res, seq, tm):
    m = res.shape[0]
    nb = seq // tm
    in_specs = []
    for width in (DIL_DIM, DIL_HD):
        for _, dil in DIL_GROUPS:
            in_specs.append(pl.BlockSpec((None, dil, tm // dil, width),
                                         lambda i: (i // nb, 0, i % nb, 0)))
    in_specs += [
        pl.BlockSpec((None, DIL_DIM, D_MODEL), lambda i: (layer, 0, 0), pipeline_mode=pl.Buffered(1)),
        pl.BlockSpec((tm, D_MODEL), lambda i: (i, 0)),
    ]
    return pl.pallas_call(
        _merge_matmul_res_kernel,
        grid=(m // tm,),
        in_specs=in_specs,
        out_specs=pl.BlockSpec((tm, D_MODEL), lambda i: (i, 0)),
        out_shape=jax.ShapeDtypeStruct((m, D_MODEL), F32),
        scratch_shapes=[
            pltpu.VMEM((N_GROUPS - 1, DIL_HEADS, tm, DIL_HD), F32),
            pltpu.VMEM((N_GROUPS - 1, tm, DIL_HD), F32),
        ],
        compiler_params=_params("parallel"),
        name="merge_matmul_res",
    )(*os, *lses, w_arr, res)


def _attn_sample_kernel(q_ref, kvn_ref, c0_ref, c1_ref, c2_ref, o_ref):
    caches = (c0_ref, c1_ref, c2_ref)
    steps = float(BAND) - lax.broadcasted_iota(jnp.int32, (BAND, 1, 1), 0).astype(F32)
    head = lax.broadcasted_iota(jnp.int32, (1, DIL_HEADS, 1), 1)
    slope = jnp.zeros((1, DIL_HEADS, 1), F32)
    for h in range(DIL_HEADS):
        slope = jnp.where(head == h, 2.0 ** -(h + 1), slope)
    bias = slope * steps
    outs, lses = [], []
    for g in range(N_GROUPS):
        c_ref = caches[g]
        q = q_ref[g]
        kn, vn = kvn_ref[g, 0], kvn_ref[g, 1]
        sb = jnp.sum(c_ref[:, 0] * q[None], axis=-1, keepdims=True) - bias
        sn = jnp.sum(kn * q, axis=-1, keepdims=True)
        m = jnp.maximum(jnp.max(sb, axis=0), sn)
        pb = jnp.exp(sb - m[None])
        pn = jnp.exp(sn - m)
        l = jnp.sum(pb, axis=0) + pn
        outs.append((jnp.sum(pb * c_ref[:, 1], axis=0) + pn * vn) / l)
        lses.append(m + jnp.log(l))
    m = jnp.maximum(jnp.maximum(lses[0], lses[1]), lses[2])
    es = [jnp.exp(a - m) for a in lses]
    inv = 1.0 / (es[0] + es[1] + es[2])
    o_ref[...] = outs[0] * (es[0] * inv) + outs[1] * (es[1] * inv) + outs[2] * (es[2] * inv)


def _attn_sample(q, kv_new, caches):
    b = q.shape[0]
    views, cache_specs = [], []
    for cache, (_, dil) in zip(caches, DIL_GROUPS):
        wb = cache.shape[1]
        views.append(cache.reshape(b, wb // dil, dil, 2, DIL_HEADS, DIL_HD))
        cache_specs.append(pl.BlockSpec((None, BAND, None, 2, DIL_HEADS, DIL_HD),
                                        lambda i: (i, 0, 0, 0, 0, 0)))
    o = pl.pallas_call(
        _attn_sample_kernel,
        grid=(b,),
        in_specs=[
            pl.BlockSpec((None, N_GROUPS, DIL_HEADS, DIL_HD), lambda i: (i, 0, 0, 0)),
            pl.BlockSpec((None, N_GROUPS, 2, DIL_HEADS, DIL_HD), lambda i: (i, 0, 0, 0, 0)),
        ] + cache_specs,
        out_specs=pl.BlockSpec((None, DIL_HEADS, DIL_HD), lambda i: (i, 0, 0)),
        out_shape=jax.ShapeDtypeStruct((b, DIL_HEADS, DIL_HD), F32),
        compiler_params=_params("parallel"),
        name="attn_sample",
    )(q.reshape(b, N_GROUPS, DIL_HEADS, DIL_HD),
      kv_new.reshape(b, N_GROUPS, 2, DIL_HEADS, DIL_HD), *views)
    return o.reshape(b, DIL_DIM)


def _trunk(x, xs, batch, seq, hg_state, caches, wts):
    (norm_g, ffn_w_in, ffn_w_out, hg_w_in, hg_w_out, lbs, hg_out_g, kv_norm_g, w_kv,
     k_norm_g, dil_w_q, dil_q_norm_g, dil_w_o) = wts
    ns = xs.shape[0]
    tm, tn = 512, 1024
    states_p, states_s = [], []
    kv_p = kv_s = None
    gain = lambda layer, idx: pl.BlockSpec((None, None, 1, D_MODEL), lambda i, j: (layer, idx, 0, 0))
    for layer in range(DEPTH):
        x, xs = _ffn(x, xs, norm_g, ffn_w_in, ffn_w_out, layer, 0, 0)
        if layer < N_A_LAYERS:
            w_spec = pl.BlockSpec((None, D_MODEL, tn), lambda i, j, layer=layer: (layer, 0, j))
            lb = lbs[layer].reshape(HG_HEADS, HG_DK)
            g_out = hg_out_g[layer].reshape(1, HG_DK)
            proj = _norm_matmul(x, norm_g, gain(layer, 1), hg_w_in, w_spec,
                                4 * D_MODEL, k_norm_g, "plain", 2 * tm, tn)
            o, s = _hgrn_scan(proj, lb, g_out, batch, seq)
            states_p.append(s)
            x = _matmul_res(o, hg_w_out, layer, x, tm)
            proj = _norm_matmul(xs, norm_g, gain(layer, 1), hg_w_in, w_spec,
                                4 * D_MODEL, k_norm_g, "plain", ns, tn)
            o, s = _hgrn_step(proj, hg_state, layer, lb, g_out)
            states_s.append(s)
            xs = _matmul_res(o, hg_w_out, layer, xs, ns)
        else:
            bi = layer - N_A_LAYERS
            qs = _q_streams(x, norm_g,
                            pl.BlockSpec((None, None, 1, D_MODEL), lambda i, layer=layer: (layer, 1, 0, 0)),
                            dil_w_q, bi, dil_q_norm_g[bi], batch, seq, tm)
            parts = [_attn_prompt_group(qs[gi], kv_p[2 * gi], kv_p[2 * gi + 1], gi)
                     for gi in range(N_GROUPS)]
            x = _merge_matmul_res([p[0] for p in parts], [p[1] for p in parts],
                                  dil_w_o, bi, x, seq, tm)
            w_spec = pl.BlockSpec((None, D_MODEL, DIL_DIM), lambda i, j, bi=bi: (bi, 0, j))
            q = _norm_matmul(xs, norm_g, gain(layer, 1), dil_w_q, w_spec,
                             N_GROUPS * DIL_DIM, dil_q_norm_g[bi], "q", ns, DIL_DIM)
            xs = _matmul_res(_attn_sample(q, kv_s, caches), dil_w_o, bi, xs, ns)
        x, xs = _ffn(x, xs, norm_g, ffn_w_in, ffn_w_out, layer, 1, 2)
        if layer == N_A_LAYERS - 1:
            kv_gain = kv_norm_g.reshape(1, D_MODEL)
            kv_gain_spec = pl.BlockSpec((1, D_MODEL), lambda i, j: (0, 0))
            w_spec = pl.BlockSpec((D_MODEL, DIL_DIM), lambda i, j: (0, j))
            kv_p = _kv_streams(x, kv_gain, kv_gain_spec, w_kv, w_spec, k_norm_g, batch, seq, tm)
            kv_s = _norm_matmul(xs, kv_gain, kv_gain_spec, w_kv, w_spec,
                                2 * N_GROUPS * DIL_DIM, k_norm_g, "kv", ns, DIL_DIM)
    return x, xs, jnp.stack(states_p, axis=0), jnp.stack(states_s, axis=0), kv_p, kv_s


def kernel(x_prompt, x_sample, state_hgrn, cache_kv_w128, cache_kv_w512, cache_kv_w2048,
           norm_g, ffn_w_in, ffn_w_out, hg_w_in, hg_w_out, hg_lb_logits, hg_out_g,
           kv_norm_g, w_kv, k_norm_g, dil_w_q, dil_q_norm_g, dil_w_o):
    batch, seq, _ = x_prompt.shape
    dec_batch = x_sample.shape[0]
    p = jax.nn.softmax(hg_lb_logits.astype(F32), axis=0)
    lbs = jnp.cumsum(p, axis=0) - p[0]
    wts = (norm_g.reshape(DEPTH, 3, 1, D_MODEL), ffn_w_in, ffn_w_out,
           hg_w_in.astype(BF16), hg_w_out.astype(BF16), lbs, hg_out_g, kv_norm_g,
           w_kv.astype(BF16), k_norm_g, dil_w_q.astype(BF16), dil_q_norm_g, dil_w_o.astype(BF16))

    y_p, y_s, hg_p, hg_s, kv_p, kv_s = _trunk(
        x_prompt.reshape(batch * seq, D_MODEL), x_sample.reshape(dec_batch, D_MODEL), batch, seq,
        state_hgrn, (cache_kv_w128, cache_kv_w512, cache_kv_w2048), wts)

    kv_s = kv_s.reshape(dec_batch, 1, N_GROUPS, 2, DIL_HEADS, DIL_HD)
    outs = [y_p.reshape(batch, seq, D_MODEL), y_s.reshape(dec_batch, 1, D_MODEL), hg_p, hg_s]
    for gi, (win, dil) in enumerate(DIL_GROUPS):
        rows = min(win, seq)
        tail = [jnp.swapaxes(a[:, :, (seq - rows) // dil:], 1, 2).reshape(
            batch, rows, DIL_HEADS, DIL_HD) for a in (kv_p[2 * gi], kv_p[2 * gi + 1])]
        outs.append(jnp.stack(tail, axis=2))
        outs.append(kv_s[:, :, gi])
    return tuple(outs)
```

```python
import functools

import jax
import jax.numpy as jnp
from jax import lax
from jax.experimental import pallas as pl
from jax.experimental.pallas import tpu as pltpu

D_MODEL = 2048
D_FF = 5632
DEPTH = 4
N_A_LAYERS = 2
HG_HEADS = 16
HG_DK = 128
DIL_GROUPS = ((128, 1), (512, 4), (2048, 16))
N_GROUPS = 3
DIL_HEADS = 8
DIL_HD = 128
DIL_DIM = DIL_HEADS * DIL_HD
BAND = 128
EPS = 1e-6
NEG = -1e30
F32 = jnp.float32
BF16 = jnp.bfloat16

VMEM_LIMIT_BYTES = 56 * 1024 * 1024
BF16_SUBLANES = 16
HG_CHUNK = 32
HG_TILE = 512
HG_HEADS_PER_PROGRAM = 2
HG_SAFE_DECAY = 80.0

_NT = (((1,), (1,)), ((), ()))
_TN = (((0,), (0,)), ((), ()))


def _params(*semantics):
    return pltpu.CompilerParams(dimension_semantics=semantics, vmem_limit_bytes=VMEM_LIMIT_BYTES)


def _rms_scale(x):
    return lax.rsqrt(jnp.mean(x * x, axis=-1, keepdims=True) + EPS)


def _silu(x):
    return x / (1.0 + jnp.exp(-x))


def _dot(a, b):
    return jnp.dot(a, b, preferred_element_type=F32)


def _ffn_kernel(x_ref, xs_ref, g_ref, wg_ref, wu_ref, wo_ref, o_ref, os_ref, xn_ref, accs_ref):
    j = pl.program_id(1)
    tm = x_ref.shape[0]
    ns = xs_ref.shape[0]

    @pl.when(j == 0)
    def _():
        g = g_ref[...]
        x = x_ref[...]
        xn_ref[0:tm, :] = (x * _rms_scale(x) * g).astype(BF16)
        xs = xs_ref[...]
        xsn = jnp.concatenate(
            [xs * _rms_scale(xs) * g, jnp.zeros((BF16_SUBLANES - ns, D_MODEL), F32)], axis=0)
        xn_ref[tm:tm + BF16_SUBLANES, :] = xsn.astype(BF16)
        o_ref[...] = jnp.zeros_like(o_ref)
        accs_ref[...] = jnp.zeros_like(accs_ref)

    xn = xn_ref[...]
    gate = _dot(xn, wg_ref[...].astype(BF16))
    up = _dot(xn, wu_ref[...].astype(BF16))
    act = (_silu(gate) * up).astype(BF16)
    res = _dot(act, wo_ref[...].astype(BF16))
    o_ref[...] += res[0:tm]
    accs_ref[...] += res[tm:tm + BF16_SUBLANES]

    @pl.when(j == pl.num_programs(1) - 1)
    def _():
        o_ref[...] = x_ref[...] + 0.5 * o_ref[...]
        os_ref[...] = xs_ref[...] + 0.5 * accs_ref[0:ns, :]


def _ffn(x, xs, norm_g, w_in, w_out, layer, half, gain_idx, tm=1024, tf=256):
    m = x.shape[0]
    ns = xs.shape[0]
    nj = D_FF // tf
    return pl.pallas_call(
        _ffn_kernel,
        grid=(m // tm, nj),
        in_specs=[
            pl.BlockSpec((tm, D_MODEL), lambda i, j: (i, 0)),
            pl.BlockSpec((ns, D_MODEL), lambda i, j: (0, 0)),
            pl.BlockSpec((None, None, 1, D_MODEL), lambda i, j: (layer, gain_idx, 0, 0)),
            pl.BlockSpec((None, None, D_MODEL, tf), lambda i, j: (layer, half, 0, j)),
            pl.BlockSpec((None, None, D_MODEL, tf), lambda i, j: (layer, half, 0, j + nj)),
            pl.BlockSpec((None, None, tf, D_MODEL), lambda i, j: (layer, half, j, 0)),
        ],
        out_specs=[
            pl.BlockSpec((tm, D_MODEL), lambda i, j: (i, 0)),
            pl.BlockSpec((ns, D_MODEL), lambda i, j: (0, 0)),
        ],
        out_shape=[
            jax.ShapeDtypeStruct((m, D_MODEL), F32),
            jax.ShapeDtypeStruct((ns, D_MODEL), F32),
        ],
        scratch_shapes=[
            pltpu.VMEM((tm + BF16_SUBLANES, D_MODEL), BF16),
            pltpu.VMEM((BF16_SUBLANES, D_MODEL), F32),
        ],
        compiler_params=_params("arbitrary", "arbitrary"),
        name="ffn",
    )(x, xs, norm_g, w_in, w_in, w_out)


def _head_norm(y, gain, scale):
    parts = []
    for h in range(DIL_HEADS):
        yh = y[:, h * DIL_HD:(h + 1) * DIL_HD]
        parts.append(yh * _rms_scale(yh) * gain * scale)
    return jnp.concatenate(parts, axis=-1)


def _norm_matmul_kernel(x_ref, g_ref, w_ref, hg_ref, o_ref, xn_ref, *, mode):
    j = pl.program_id(1)

    @pl.when(j == 0)
    def _():
        x = x_ref[...]
        xn_ref[...] = (x * _rms_scale(x) * g_ref[...]).astype(BF16)

    y = _dot(xn_ref[...], w_ref[...].astype(BF16))
    if mode == "plain":
        o_ref[...] = y
    elif mode == "q":
        o_ref[...] = _head_norm(y, hg_ref[pl.ds(j, 1), :], DIL_HD ** -0.5)
    else:
        @pl.when(j % 2 == 0)
        def _():
            o_ref[...] = _head_norm(y, hg_ref[pl.ds(j // 2, 1), :], 1.0)

        @pl.when(j % 2 == 1)
        def _():
            o_ref[...] = y


def _norm_matmul(x, gain_arr, gain_spec, w_arr, w_spec, n, head_gain, mode, tm, tn):
    m = x.shape[0]
    return pl.pallas_call(
        functools.partial(_norm_matmul_kernel, mode=mode),
        grid=(m // tm, n // tn),
        in_specs=[
            pl.BlockSpec((tm, D_MODEL), lambda i, j: (i, 0)),
            gain_spec,
            w_spec,
            pl.BlockSpec(head_gain.shape, lambda i, j: (0, 0)),
        ],
        out_specs=pl.BlockSpec((tm, tn), lambda i, j: (i, j)),
        out_shape=jax.ShapeDtypeStruct((m, n), F32),
        scratch_shapes=[pltpu.VMEM((tm, D_MODEL), BF16)],
        compiler_params=_params("parallel", "arbitrary"),
        name="norm_matmul_" + mode,
    )(x, gain_arr, w_arr, head_gain)


def _kv_streams_kernel(x_ref, g_ref, w_ref, hg_ref, *refs):
    outs = refs[:2 * N_GROUPS]
    y_ref = refs[2 * N_GROUPS]
    tm = x_ref.shape[0]
    x = x_ref[...]
    xn = (x * _rms_scale(x) * g_ref[...]).astype(BF16)
    n_direct = 2 * sum(1 for _, dil in DIL_GROUPS if dil == 1)
    for jj in range(2 * N_GROUPS):
        group = jj // 2
        dil = DIL_GROUPS[group][1]
        out = outs[jj]
        y = _dot(xn, w_ref[:, jj * DIL_DIM:(jj + 1) * DIL_DIM])
        val = _head_norm(y, hg_ref[group:group + 1, :], 1.0) if jj % 2 == 0 else y
        if dil == 1:
            out[0] = val
        else:
            slot = jj - n_direct
            for h in range(DIL_HEADS):
                y_ref[slot, h] = val[:, h * DIL_HD:(h + 1) * DIL_HD]
            for r in range(dil):
                for h in range(DIL_HEADS):
                    out[r, :, h * DIL_HD:(h + 1) * DIL_HD] = y_ref[
                        slot, h, pl.ds(r, tm // dil, stride=dil), :]


def _kv_streams(x, gain_arr, w_arr, head_gain, batch, seq, tm):
    nb = seq // tm
    n_out = 2 * N_GROUPS
    out_specs, out_shapes = [], []
    for jj in range(n_out):
        dil = DIL_GROUPS[jj // 2][1]
        out_specs.append(pl.BlockSpec((None, dil, tm // dil, DIL_DIM),
                                      lambda i: (i // nb, 0, i % nb, 0)))
        out_shapes.append(jax.ShapeDtypeStruct((batch, dil, seq // dil, DIL_DIM), F32))
    n_strided = n_out - 2 * sum(1 for _, dil in DIL_GROUPS if dil == 1)
    return pl.pallas_call(
        _kv_streams_kernel,
        grid=(batch * nb,),
        in_specs=[
            pl.BlockSpec((tm, D_MODEL), lambda i: (i, 0)),
            pl.BlockSpec((1, D_MODEL), lambda i: (0, 0)),
            pl.BlockSpec((D_MODEL, n_out * DIL_DIM), lambda i: (0, 0), pipeline_mode=pl.Buffered(1)),
            pl.BlockSpec(head_gain.shape, lambda i: (0, 0)),
        ],
        out_specs=out_specs,
        out_shape=out_shapes,
        scratch_shapes=[pltpu.VMEM((n_strided, DIL_HEADS, tm, DIL_HD), F32)],
        compiler_params=_params("parallel"),
        name="kv_streams",
    )(x, gain_arr, w_arr, head_gain)


def _q_streams_kernel(x_ref, g_ref, w_ref, hg_ref, o0_ref, o1_ref, o2_ref, y_ref):
    tm = x_ref.shape[0]
    x = x_ref[...]
    xn = (x * _rms_scale(x) * g_ref[...]).astype(BF16)
    for group, out in enumerate((o0_ref, o1_ref, o2_ref)):
        dil = DIL_GROUPS[group][1]
        y = _dot(xn, w_ref[:, group * DIL_DIM:(group + 1) * DIL_DIM].astype(BF16))
        val = _head_norm(y, hg_ref[group:group + 1, :], DIL_HD ** -0.5)
        if dil == 1:
            out[0] = val.astype(out.dtype)
        else:
            for h in range(DIL_HEADS):
                y_ref[group - 1, h] = val[:, h * DIL_HD:(h + 1) * DIL_HD]
            for r in range(dil):
                for h in range(DIL_HEADS):
                    out[r, :, h * DIL_HD:(h + 1) * DIL_HD] = y_ref[
                        group - 1, h, pl.ds(r, tm // dil, stride=dil), :].astype(out.dtype)


def _q_streams(x, gain_arr, gain_spec, w_arr, bi, head_gain, batch, seq, tm):
    nb = seq // tm
    out_specs, out_shapes = [], []
    for _, dil in DIL_GROUPS:
        out_specs.append(pl.BlockSpec((None, dil, tm // dil, DIL_DIM),
                                      lambda i: (i // nb, 0, i % nb, 0)))
        out_shapes.append(jax.ShapeDtypeStruct((batch, dil, seq // dil, DIL_DIM), BF16))
    return pl.pallas_call(
        _q_streams_kernel,
        grid=(batch * nb,),
        in_specs=[
            pl.BlockSpec((tm, D_MODEL), lambda i: (i, 0)),
            gain_spec,
            pl.BlockSpec((None, D_MODEL, N_GROUPS * DIL_DIM), lambda i: (bi, 0, 0),
                         pipeline_mode=pl.Buffered(1)),
            pl.BlockSpec(head_gain.shape, lambda i: (0, 0)),
        ],
        out_specs=out_specs,
        out_shape=out_shapes,
        scratch_shapes=[pltpu.VMEM((N_GROUPS - 1, DIL_HEADS, tm, DIL_HD), F32)],
        compiler_params=_params("parallel"),
        name="q_streams",
    )(x, gain_arr, w_arr, head_gain)


def _matmul_res_kernel(a_ref, w_ref, r_ref, o_ref):
    o_ref[...] = r_ref[...] + _dot(a_ref[...].astype(BF16), w_ref[...].astype(BF16))


def _matmul_res(a, w_arr, layer, res, tm):
    m, k = a.shape
    n = res.shape[1]
    return pl.pallas_call(
        _matmul_res_kernel,
        grid=(m // tm,),
        in_specs=[
            pl.BlockSpec((tm, k), lambda i: (i, 0)),
            pl.BlockSpec((None, k, n), lambda i: (layer, 0, 0), pipeline_mode=pl.Buffered(1)),
            pl.BlockSpec((tm, n), lambda i: (i, 0)),
        ],
        out_specs=pl.BlockSpec((tm, n), lambda i: (i, 0)),
        out_shape=jax.ShapeDtypeStruct((m, n), F32),
        compiler_params=_params("parallel"),
        name="matmul_res",
    )(a, w_arr, res)


def _hgrn_gates(z, lb):
    e = jnp.exp(-jnp.abs(z))
    inv = 1.0 / (1.0 + e)
    pos = z >= 0
    sig = jnp.where(pos, inv, e * inv)
    sig_neg = jnp.where(pos, e * inv, inv)
    k = (1.0 - lb) * sig_neg
    f = lb + (1.0 - lb) * sig
    return f, k, jnp.log(jnp.maximum(f, 1e-30))


def _hgrn_pairwise_chunk(q, k, v, cum):
    c = q.shape[0]
    n_sub = c // 8
    sub_row = lax.broadcasted_iota(jnp.int32, (8, 1), 0)
    acc = [jnp.zeros((8, HG_DK), F32) for _ in range(n_sub)]
    for s in range(c):
        cs = cum[s:s + 1, :]
        ks = k[s:s + 1, :]
        vs = v[s:s + 1, :]
        for tg in range(s // 8, n_sub):
            rows = slice(tg * 8, (tg + 1) * 8)
            p = q[rows, :] * ks * jnp.exp(cum[rows, :] - cs)
            a = jnp.sum(p, axis=-1, keepdims=True)
            if tg == s // 8:
                a = jnp.where(sub_row >= (s - tg * 8), a, 0.0)
            acc[tg] = acc[tg] + a * vs
    return jnp.concatenate(acc, axis=0)


def _hgrn_scan_kernel(q_ref, z_ref, v_ref, gt_ref, lb_ref, go_ref, o_ref, s_ref,
                      st_ref, oi_ref, k_ref, cum_ref):
    c = HG_CHUNK
    tile = HG_TILE
    n_chunks = tile // c
    n_par = HG_HEADS_PER_PROGRAM
    hp = pl.program_id(1)
    lbs = [lb_ref[pl.ds(hp * n_par + p, 1), :] for p in range(n_par)]
    lanes = [slice(p * HG_DK, (p + 1) * HG_DK) for p in range(n_par)]
    heads = range(n_par)
    g_out = go_ref[...]
    row = lax.broadcasted_iota(jnp.int32, (tile, tile), 0)
    col = lax.broadcasted_iota(jnp.int32, (tile, tile), 1)
    shift = c.bit_length() - 1
    same_chunk = jnp.right_shift(row, shift) == jnp.right_shift(col, shift)
    tri = jnp.where(jnp.logical_and(same_chunk, col <= row), 1.0, 0.0).astype(BF16)
    crow = lax.broadcasted_iota(jnp.int32, (c, c), 0)
    ccol = lax.broadcasted_iota(jnp.int32, (c, c), 1)
    causal = ccol <= crow
    st_ref[...] = jnp.zeros_like(st_ref)

    def body(ti, carry):
        r0 = pl.multiple_of(ti * tile, tile)
        rows = pl.ds(r0, tile)
        chunks = [slice(ci * c, (ci + 1) * c) for ci in range(n_chunks)]
        q = [q_ref[rows, ln] for ln in lanes]
        vb = [v_ref[rows, ln].astype(BF16) for ln in lanes]
        k, split = [], []
        for p in heads:
            _, kp, lf = _hgrn_gates(z_ref[rows, lanes[p]], lbs[p])
            k.append(kp)
            hi = lf.astype(BF16)
            rem = lf - hi.astype(F32)
            mid = rem.astype(BF16)
            split += [hi, mid, (rem - mid.astype(F32)).astype(BF16)]
        cum_all = _dot(tri, jnp.concatenate(split, axis=1))
        cum = [cum_all[:, (3 * p) * HG_DK:(3 * p + 1) * HG_DK]
               + cum_all[:, (3 * p + 1) * HG_DK:(3 * p + 2) * HG_DK]
               + cum_all[:, (3 * p + 2) * HG_DK:(3 * p + 3) * HG_DK] for p in heads]
        lasts = [[cum[p][sl.stop - 1:sl.stop, :] for sl in chunks] for p in heads]
        qdb = [(q[p] * jnp.exp(cum[p])).astype(BF16) for p in heads]
        khb = [(k[p] * jnp.exp(-cum[p])).astype(BF16) for p in heads]
        atts = [[lax.dot_general(qdb[p][sl], khb[p][sl], _NT, preferred_element_type=F32)
                 for sl in chunks] for p in heads]
        atts = [[jnp.where(causal, att, 0.0).astype(BF16) for att in atts[p]] for p in heads]
        intra = [[_dot(att, vb[p][sl]) for att, sl in zip(atts[p], chunks)] for p in heads]
        kdb = [jnp.concatenate([k[p][sl] * jnp.exp(last - cum[p][sl])
                                for sl, last in zip(chunks, lasts[p])], axis=0).astype(BF16)
               for p in heads]
        incs = [[lax.dot_general(vb[p][sl], kdb[p][sl], _TN, preferred_element_type=F32)
                 for sl in chunks] for p in heads]
        sts = [[st_ref[p]] for p in heads]
        for p in heads:
            for inc, last in zip(incs[p], lasts[p]):
                sts[p].append(sts[p][-1] * jnp.exp(last) + inc)
            st_ref[p] = sts[p][-1]
        inter = [jnp.concatenate(
            [lax.dot_general(qdb[p][sl], st.astype(BF16), _NT, preferred_element_type=F32)
             for sl, st in zip(chunks, sts[p])], axis=0) for p in heads]
        for p in heads:
            oi_ref[p] = jnp.concatenate(intra[p], axis=0) + inter[p]

        cum_min = cum[0]
        for p in heads[1:]:
            cum_min = jnp.minimum(cum_min, cum[p])

        @pl.when(jnp.min(cum_min) <= -HG_SAFE_DECAY)
        def _():
            for p in heads:
                k_ref[...] = k[p]
                cum_ref[...] = cum[p]
                oi_ref[p] = inter[p]

                def chunk(ci, carry2, p=p):
                    c0 = pl.multiple_of(ci * c, c)
                    oi_ref[p, pl.ds(c0, c), :] += _hgrn_pairwise_chunk(
                        q_ref[pl.ds(r0 + c0, c), lanes[p]], k_ref[pl.ds(c0, c), :],
                        v_ref[pl.ds(r0 + c0, c), lanes[p]], cum_ref[pl.ds(c0, c), :])
                    return carry2

                lax.fori_loop(0, n_chunks, chunk, 0)

        for p in heads:
            o = oi_ref[p]
            o = o * _rms_scale(o) * g_out * _silu(gt_ref[rows, lanes[p]])
            o_ref[rows, lanes[p]] = o.astype(o_ref.dtype)
        return carry

    lax.fori_loop(0, q_ref.shape[0] // tile, body, 0)
    for p in heads:
        s_ref[p] = st_ref[p].T


def _hgrn_scan(proj, lb, g_out, batch, seq):
    n_par = HG_HEADS_PER_PROGRAM
    n_prog = HG_HEADS // n_par
    width = n_par * HG_DK
    col = lambda k: (lambda b, h: (b, k * n_prog + h))
    return pl.pallas_call(
        _hgrn_scan_kernel,
        grid=(batch, n_prog),
        in_specs=[
            pl.BlockSpec((seq, width), col(0)),
            pl.BlockSpec((seq, width), col(1)),
            pl.BlockSpec((seq, width), col(2)),
            pl.BlockSpec((seq, width), col(3)),
            pl.BlockSpec((HG_HEADS, HG_DK), lambda b, h: (0, 0)),
            pl.BlockSpec((1, HG_DK), lambda b, h: (0, 0)),
        ],
        out_specs=[
            pl.BlockSpec((seq, width), lambda b, h: (b, h)),
            pl.BlockSpec((None, n_par, HG_DK, HG_DK), lambda b, h: (b, h, 0, 0)),
        ],
        out_shape=[
            jax.ShapeDtypeStruct((batch * seq, D_MODEL), BF16),
            jax.ShapeDtypeStruct((batch, HG_HEADS, HG_DK, HG_DK), F32),
        ],
        scratch_shapes=[
            pltpu.VMEM((n_par, HG_DK, HG_DK), F32),
            pltpu.VMEM((n_par, HG_TILE, HG_DK), F32),
            pltpu.VMEM((HG_TILE, HG_DK), F32),
            pltpu.VMEM((HG_TILE, HG_DK), F32),
        ],
        compiler_params=_params("parallel", "parallel"),
        name="hgrn_scan",
    )(proj, proj, proj, proj, lb, g_out)


def _hgrn_step_kernel(x_ref, s0_ref, lb_ref, go_ref, o_ref, s_ref):
    nh = HG_HEADS
    x = x_ref[...]
    q = x[0:nh]
    f, k, _ = _hgrn_gates(x[nh:2 * nh], lb_ref[...])
    r = lax.broadcasted_iota(jnp.int32, (HG_DK, HG_DK), 0)
    c = lax.broadcasted_iota(jnp.int32, (HG_DK, HG_DK), 1)
    eye = (r == c).astype(F32)
    cols = lax.dot_general(eye, jnp.concatenate([q, f, k], axis=0), _NT,
                           preferred_element_type=F32, precision=lax.Precision.HIGHEST)
    g_out = go_ref[...]
    for h in range(nh):
        qc = cols[:, h:h + 1]
        fc = cols[:, nh + h:nh + h + 1]
        kc = cols[:, 2 * nh + h:2 * nh + h + 1]
        v = x[2 * nh + h:2 * nh + h + 1, :]
        s_new = fc * s0_ref[h] + kc * v
        s_ref[h] = s_new
        o = jnp.sum(s_new * qc, axis=0, keepdims=True)
        o = o * _rms_scale(o) * g_out * _silu(x[3 * nh + h:3 * nh + h + 1, :])
        o_ref[:, h * HG_DK:(h + 1) * HG_DK] = o


def _hgrn_step(proj, state, layer, lb, g_out):
    b = proj.shape[0]
    nh = HG_HEADS
    o, s = pl.pallas_call(
        _hgrn_step_kernel,
        grid=(b,),
        in_specs=[
            pl.BlockSpec((None, 4 * nh, HG_DK), lambda i: (i, 0, 0)),
            pl.BlockSpec((None, None, nh, HG_DK, HG_DK), lambda i: (layer, i, 0, 0, 0)),
            pl.BlockSpec((nh, HG_DK), lambda i: (0, 0)),
            pl.BlockSpec((1, HG_DK), lambda i: (0, 0)),
        ],
        out_specs=[
            pl.BlockSpec((None, 1, D_MODEL), lambda i: (i, 0, 0)),
            pl.BlockSpec((None, nh, HG_DK, HG_DK), lambda i: (i, 0, 0, 0)),
        ],
        out_shape=[
            jax.ShapeDtypeStruct((b, 1, D_MODEL), F32),
            jax.ShapeDtypeStruct((b, nh, HG_DK, HG_DK), F32),
        ],
        compiler_params=_params("parallel"),
        name="hgrn_step",
    )(proj.reshape(b, 4 * nh, HG_DK), state, lb, g_out)
    return o.reshape(b, D_MODEL), s


def _attn_prompt_kernel(q_ref, k_ref, v_ref, o_ref, lse_ref, kp_ref, vp_ref):
    ut = pl.program_id(2)

    @pl.when(ut == 0)
    def _():
        kp_ref[...] = jnp.zeros_like(kp_ref)
        vp_ref[...] = jnp.zeros_like(vp_ref)

    qi = lax.broadcasted_iota(jnp.int32, (BAND, BAND), 0)
    kj = lax.broadcasted_iota(jnp.int32, (BAND, BAND), 1)
    steps_c = (qi - kj).astype(F32)
    steps_p = steps_c + float(BAND)
    valid_c = kj <= qi
    valid_p = kj >= qi
    lane = lax.broadcasted_iota(jnp.int32, (BAND, DIL_HD), 1)
    heads = [slice(h * DIL_HD, (h + 1) * DIL_HD) for h in range(DIL_HEADS)]
    n_sub = q_ref.shape[0] // BAND
    subs = [slice(t * BAND, (t + 1) * BAND) for t in range(n_sub)]
    q = q_ref[...]
    kall, vall = k_ref[...].astype(BF16), v_ref[...].astype(BF16)
    kprev = [kp_ref[...]] + [kall[t] for t in subs[:-1]]
    vprev = [vp_ref[...]] + [vall[t] for t in subs[:-1]]
    sc = [[lax.dot_general(q[t, sl], kall[t, sl], _NT, preferred_element_type=F32) for sl in heads]
          for t in subs]
    sp = [[lax.dot_general(q[t, sl], kprev[i][:, sl], _NT, preferred_element_type=F32)
           for sl in heads] for i, t in enumerate(subs)]
    pcs, pps, ms = [], [], []
    for i in range(n_sub):
        ok_p = jnp.logical_and(valid_p, ut > 0) if i == 0 else valid_p
        for h in range(DIL_HEADS):
            slope = 2.0 ** -(h + 1)
            c = jnp.where(valid_c, sc[i][h] - slope * steps_c, NEG)
            p = jnp.where(ok_p, sp[i][h] - slope * steps_p, NEG)
            m = jnp.max(jnp.maximum(c, p), axis=-1, keepdims=True)
            pcs.append(jnp.exp(c - m).astype(BF16))
            pps.append(jnp.exp(p - m).astype(BF16))
            ms.append(m)
    ones = jnp.ones((BAND, DIL_HD), BF16)
    ext = [_dot(pcs[i * DIL_HEADS + h], jnp.concatenate([vall[t, sl], ones], axis=1))
           + _dot(pps[i * DIL_HEADS + h], jnp.concatenate([vprev[i][:, sl], ones], axis=1))
           for i, t in enumerate(subs) for h, sl in enumerate(heads)]
    for i, t in enumerate(subs):
        lse_all = jnp.zeros((BAND, DIL_HD), F32)
        for h, sl in enumerate(heads):
            e = ext[i * DIL_HEADS + h]
            l = e[:, DIL_HD:]
            o_ref[t, sl] = e[:, :DIL_HD] / l
            lse_all = jnp.where(lane == h, ms[i * DIL_HEADS + h] + jnp.log(l), lse_all)
        lse_ref[t, :] = lse_all
    kp_ref[...] = kall[subs[-1]]
    vp_ref[...] = vall[subs[-1]]


def _attn_prompt_group(q, k, v, gi):
    batch, dil, ls, _ = q.shape
    rows = 2 * BAND
    assert ls % rows == 0, (ls, rows)
    tile = pl.BlockSpec((None, None, rows, DIL_DIM), lambda b, r, u: (b, r, u, 0))
    return pl.pallas_call(
        _attn_prompt_kernel,
        grid=(batch, dil, ls // rows),
        in_specs=[tile, tile, tile],
        out_specs=[tile, pl.BlockSpec((None, None, rows, DIL_HD), lambda b, r, u: (b, r, u, 0))],
        out_shape=[
            jax.ShapeDtypeStruct((batch, dil, ls, DIL_DIM), F32),
            jax.ShapeDtypeStruct((batch, dil, ls, DIL_HD), F32),
        ],
        scratch_shapes=[pltpu.VMEM((BAND, DIL_DIM), BF16), pltpu.VMEM((BAND, DIL_DIM), BF16)],
        compiler_params=_params("parallel", "parallel", "arbitrary"),
        name="attn_prompt_g%d" % gi,
    )(q, k, v)


def _merge_matmul_res_kernel(o0, o1, o2, l0, l1, l2, w_ref, r_ref, out_ref, ot_ref, lt_ref):
    tm = out_ref.shape[0]
    for gi, (o_ref, l_ref) in enumerate(((o1, l1), (o2, l2))):
        dil = DIL_GROUPS[gi + 1][1]
        for r in range(dil):
            for h in range(DIL_HEADS):
                ot_ref[gi, h, pl.ds(r, tm // dil, stride=dil), :] = o_ref[
                    r, :, h * DIL_HD:(h + 1) * DIL_HD]
            lt_ref[gi, pl.ds(r, tm // dil, stride=dil), :] = l_ref[r]
    a0, a1, a2 = l0[0], lt_ref[0], lt_ref[1]
    m = jnp.maximum(jnp.maximum(a0, a1), a2)
    e0, e1, e2 = jnp.exp(a0 - m), jnp.exp(a1 - m), jnp.exp(a2 - m)
    inv = 1.0 / (e0 + e1 + e2)
    w0, w1, w2 = e0 * inv, e1 * inv, e2 * inv
    parts = []
    for h in range(DIL_HEADS):
        sl = slice(h * DIL_HD, (h + 1) * DIL_HD)
        parts.append((o0[0, :, sl] * w0[:, h:h + 1] + ot_ref[0, h] * w1[:, h:h + 1]
                      + ot_ref[1, h] * w2[:, h:h + 1]).astype(BF16))
    out_ref[...] = r_ref[...] + _dot(jnp.concatenate(parts, axis=1), w_ref[...].astype(BF16))


def _merge_matmul_res(os, lses, w_arr, layer, res, seq, tm):
    m = res.shape[0]
    nb = seq // tm
    in_specs = []
    for width in (DIL_DIM, DIL_HD):
        for _, dil in DIL_GROUPS:
            in_specs.append(pl.BlockSpec((None, dil, tm // dil, width),
                                         lambda i: (i // nb, 0, i % nb, 0)))
    in_specs += [
        pl.BlockSpec((None, DIL_DIM, D_MODEL), lambda i: (layer, 0, 0), pipeline_mode=pl.Buffered(1)),
        pl.BlockSpec((tm, D_MODEL), lambda i: (i, 0)),
    ]
    return pl.pallas_call(
        _merge_matmul_res_kernel,
        grid=(m // tm,),
        in_specs=in_specs,
        out_specs=pl.BlockSpec((tm, D_MODEL), lambda i: (i, 0)),
        out_shape=jax.ShapeDtypeStruct((m, D_MODEL), F32),
        scratch_shapes=[
            pltpu.VMEM((N_GROUPS - 1, DIL_HEADS, tm, DIL_HD), F32),
            pltpu.VMEM((N_GROUPS - 1, tm, DIL_HD), F32),
        ],
        compiler_params=_params("parallel"),
        name="merge_matmul_res",
    )(*os, *lses, w_arr, res)


def _attn_sample_kernel(q_ref, kvn_ref, c0_ref, c1_ref, c2_ref, o_ref):
    caches = (c0_ref, c1_ref, c2_ref)
    steps = float(BAND) - lax.broadcasted_iota(jnp.int32, (BAND, 1, 1), 0).astype(F32)
    head = lax.broadcasted_iota(jnp.int32, (1, DIL_HEADS, 1), 1)
    slope = jnp.zeros((1, DIL_HEADS, 1), F32)
    for h in range(DIL_HEADS):
        slope = jnp.where(head == h, 2.0 ** -(h + 1), slope)
    bias = slope * steps
    outs, lses = [], []
    for g in range(N_GROUPS):
        c_ref = caches[g]
        q = q_ref[g]
        kn, vn = kvn_ref[g, 0], kvn_ref[g, 1]
        sb = jnp.sum(c_ref[:, 0] * q[None], axis=-1, keepdims=True) - bias
        sn = jnp.sum(kn * q, axis=-1, keepdims=True)
        m = jnp.maximum(jnp.max(sb, axis=0), sn)
        pb = jnp.exp(sb - m[None])
        pn = jnp.exp(sn - m)
        l = jnp.sum(pb, axis=0) + pn
        outs.append((jnp.sum(pb * c_ref[:, 1], axis=0) + pn * vn) / l)
        lses.append(m + jnp.log(l))
    m = jnp.maximum(jnp.maximum(lses[0], lses[1]), lses[2])
    es = [jnp.exp(a - m) for a in lses]
    inv = 1.0 / (es[0] + es[1] + es[2])
    o_ref[...] = outs[0] * (es[0] * inv) + outs[1] * (es[1] * inv) + outs[2] * (es[2] * inv)


def _attn_sample(q, kv_new, caches):
    b = q.shape[0]
    views, cache_specs = [], []
    for cache, (_, dil) in zip(caches, DIL_GROUPS):
        wb = cache.shape[1]
        views.append(cache.reshape(b, wb // dil, dil, 2, DIL_HEADS, DIL_HD))
        cache_specs.append(pl.BlockSpec((None, BAND, None, 2, DIL_HEADS, DIL_HD),
                                        lambda i: (i, 0, 0, 0, 0, 0)))
    o = pl.pallas_call(
        _attn_sample_kernel,
        grid=(b,),
        in_specs=[
            pl.BlockSpec((None, N_GROUPS, DIL_HEADS, DIL_HD), lambda i: (i, 0, 0, 0)),
            pl.BlockSpec((None, N_GROUPS, 2, DIL_HEADS, DIL_HD), lambda i: (i, 0, 0, 0, 0)),
        ] + cache_specs,
        out_specs=pl.BlockSpec((None, DIL_HEADS, DIL_HD), lambda i: (i, 0, 0)),
        out_shape=jax.ShapeDtypeStruct((b, DIL_HEADS, DIL_HD), F32),
        compiler_params=_params("parallel"),
        name="attn_sample",
    )(q.reshape(b, N_GROUPS, DIL_HEADS, DIL_HD),
      kv_new.reshape(b, N_GROUPS, 2, DIL_HEADS, DIL_HD), *views)
    return o.reshape(b, DIL_DIM)


def _trunk(x, xs, batch, seq, hg_state, caches, wts):
    (norm_g, ffn_w_in, ffn_w_out, hg_w_in, hg_w_out, lbs, hg_out_g, kv_norm_g, w_kv,
     k_norm_g, dil_w_q, dil_q_norm_g, dil_w_o) = wts
    ns = xs.shape[0]
    tm, tn = 512, 1024
    states_p, states_s = [], []
    kv_p = kv_s = None
    gain = lambda layer, idx: pl.BlockSpec((None, None, 1, D_MODEL), lambda i, j: (layer, idx, 0, 0))
    for layer in range(DEPTH):
        x, xs = _ffn(x, xs, norm_g, ffn_w_in, ffn_w_out, layer, 0, 0)
        if layer < N_A_LAYERS:
            w_spec = pl.BlockSpec((None, D_MODEL, tn), lambda i, j, layer=layer: (layer, 0, j))
            lb = lbs[layer].reshape(HG_HEADS, HG_DK)
            g_out = hg_out_g[layer].reshape(1, HG_DK)
            proj = _norm_matmul(x, norm_g, gain(layer, 1), hg_w_in, w_spec,
                                4 * D_MODEL, k_norm_g, "plain", 2 * tm, tn)
            o, s = _hgrn_scan(proj, lb, g_out, batch, seq)
            states_p.append(s)
            x = _matmul_res(o, hg_w_out, layer, x, tm)
            proj = _norm_matmul(xs, norm_g, gain(layer, 1), hg_w_in, w_spec,
                                4 * D_MODEL, k_norm_g, "plain", ns, tn)
            o, s = _hgrn_step(proj, hg_state, layer, lb, g_out)
            states_s.append(s)
            xs = _matmul_res(o, hg_w_out, layer, xs, ns)
        else:
            bi = layer - N_A_LAYERS
            qs = _q_streams(x, norm_g,
                            pl.BlockSpec((None, None, 1, D_MODEL), lambda i, layer=layer: (layer, 1, 0, 0)),
                            dil_w_q, bi, dil_q_norm_g[bi], batch, seq, tm)
            parts = [_attn_prompt_group(qs[gi], kv_p[2 * gi], kv_p[2 * gi + 1], gi)
                     for gi in range(N_GROUPS)]
            x = _merge_matmul_res([p[0] for p in parts], [p[1] for p in parts],
                                  dil_w_o, bi, x, seq, tm)
            w_spec = pl.BlockSpec((None, D_MODEL, DIL_DIM), lambda i, j, bi=bi: (bi, 0, j))
            q = _norm_matmul(xs, norm_g, gain(layer, 1), dil_w_q, w_spec,
                             N_GROUPS * DIL_DIM, dil_q_norm_g[bi], "q", ns, DIL_DIM)
            xs = _matmul_res(_attn_sample(q, kv_s, caches), dil_w_o, bi, xs, ns)
        x, xs = _ffn(x, xs, norm_g, ffn_w_in, ffn_w_out, layer, 1, 2)
        if layer == N_A_LAYERS - 1:
            kv_gain = kv_norm_g.reshape(1, D_MODEL)
            kv_gain_spec = pl.BlockSpec((1, D_MODEL), lambda i, j: (0, 0))
            w_spec = pl.BlockSpec((D_MODEL, DIL_DIM), lambda i, j: (0, j))
            kv_p = _kv_streams(x, kv_gain, w_kv, k_norm_g, batch, seq, tm // 2)
            kv_s = _norm_matmul(xs, kv_gain, kv_gain_spec, w_kv, w_spec,
                                2 * N_GROUPS * DIL_DIM, k_norm_g, "kv", ns, DIL_DIM)
    return x, xs, jnp.stack(states_p, axis=0), jnp.stack(states_s, axis=0), kv_p, kv_s


def kernel(x_prompt, x_sample, state_hgrn, cache_kv_w128, cache_kv_w512, cache_kv_w2048,
           norm_g, ffn_w_in, ffn_w_out, hg_w_in, hg_w_out, hg_lb_logits, hg_out_g,
           kv_norm_g, w_kv, k_norm_g, dil_w_q, dil_q_norm_g, dil_w_o):
    batch, seq, _ = x_prompt.shape
    dec_batch = x_sample.shape[0]
    p = jax.nn.softmax(hg_lb_logits.astype(F32), axis=0)
    lbs = jnp.cumsum(p, axis=0) - p[0]
    wts = (norm_g.reshape(DEPTH, 3, 1, D_MODEL), ffn_w_in, ffn_w_out,
           hg_w_in.astype(BF16), hg_w_out, lbs, hg_out_g, kv_norm_g,
           w_kv.astype(BF16), k_norm_g, dil_w_q, dil_q_norm_g, dil_w_o)

    y_p, y_s, hg_p, hg_s, kv_p, kv_s = _trunk(
        x_prompt.reshape(batch * seq, D_MODEL), x_sample.reshape(dec_batch, D_MODEL), batch, seq,
        state_hgrn, (cache_kv_w128, cache_kv_w512, cache_kv_w2048), wts)

    kv_s = kv_s.reshape(dec_batch, 1, N_GROUPS, 2, DIL_HEADS, DIL_HD)
    outs = [y_p.reshape(batch, seq, D_MODEL), y_s.reshape(dec_batch, 1, D_MODEL), hg_p, hg_s]
    for gi, (win, dil) in enumerate(DIL_GROUPS):
        rows = min(win, seq)
        tail = [jnp.swapaxes(a[:, :, (seq - rows) // dil:], 1, 2).reshape(
            batch, rows, DIL_HEADS, DIL_HD) for a in (kv_p[2 * gi], kv_p[2 * gi + 1])]
        outs.append(jnp.stack(tail, axis=2))
        outs.append(kv_s[:, :, gi])
    return tuple(outs)
```

```python
import functools

import jax
import jax.numpy as jnp
from jax import lax
from jax.experimental import pallas as pl
from jax.experimental.pallas import tpu as pltpu

D_MODEL = 2048
D_FF = 5632
DEPTH = 4
N_A_LAYERS = 2
HG_HEADS = 16
HG_DK = 128
DIL_GROUPS = ((128, 1), (512, 4), (2048, 16))
N_GROUPS = 3
DIL_HEADS = 8
DIL_HD = 128
DIL_DIM = DIL_HEADS * DIL_HD
BAND = 128
EPS = 1e-6
NEG = -1e30
F32 = jnp.float32
BF16 = jnp.bfloat16

VMEM_LIMIT_BYTES = 56 * 1024 * 1024
BF16_SUBLANES = 16
HG_CHUNK = 32
HG_TILE = 512
HG_HEADS_PER_PROGRAM = 2
HG_CUM_BLOCK = 256
MERGE_SUB_ROWS = 256
HG_SAFE_DECAY = 80.0

_NT = (((1,), (1,)), ((), ()))
_TN = (((0,), (0,)), ((), ()))


def _params(*semantics):
    return pltpu.CompilerParams(dimension_semantics=semantics, vmem_limit_bytes=VMEM_LIMIT_BYTES)


def _rms_scale(x):
    return lax.rsqrt(jnp.mean(x * x, axis=-1, keepdims=True) + EPS)


def _silu(x):
    return x / (1.0 + jnp.exp(-x))


def _dot(a, b):
    return jnp.dot(a, b, preferred_element_type=F32)


def _ffn_kernel(x_ref, xs_ref, g_ref, wg_ref, wu_ref, wo_ref, o_ref, os_ref, xn_ref, accs_ref):
    j = pl.program_id(1)
    tm = x_ref.shape[0]
    ns = xs_ref.shape[0]

    @pl.when(j == 0)
    def _():
        g = g_ref[...]
        x = x_ref[...]
        xn_ref[0:tm, :] = (x * _rms_scale(x) * g).astype(BF16)
        xs = xs_ref[...]
        xsn = jnp.concatenate(
            [xs * _rms_scale(xs) * g, jnp.zeros((BF16_SUBLANES - ns, D_MODEL), F32)], axis=0)
        xn_ref[tm:tm + BF16_SUBLANES, :] = xsn.astype(BF16)
        o_ref[...] = jnp.zeros_like(o_ref)
        accs_ref[...] = jnp.zeros_like(accs_ref)

    def step(rows):
        xn = xn_ref[0:rows, :]
        gate = _dot(xn, wg_ref[...].astype(BF16))
        up = _dot(xn, wu_ref[...].astype(BF16))
        act = (_silu(gate) * up).astype(BF16)
        return _dot(act, wo_ref[...].astype(BF16))

    first_tile = pl.program_id(0) == 0

    @pl.when(first_tile)
    def _():
        res = step(tm + BF16_SUBLANES)
        o_ref[...] += res[0:tm]
        accs_ref[...] += res[tm:tm + BF16_SUBLANES]

    @pl.when(jnp.logical_not(first_tile))
    def _():
        o_ref[...] += step(tm)

    last = j == pl.num_programs(1) - 1

    @pl.when(last)
    def _():
        o_ref[...] = x_ref[...] + 0.5 * o_ref[...]

    @pl.when(jnp.logical_and(last, first_tile))
    def _():
        os_ref[...] = xs_ref[...] + 0.5 * accs_ref[0:ns, :]


def _ffn(x, xs, norm_g, w_in, w_out, layer, half, gain_idx, tm=1024, tf=256):
    m = x.shape[0]
    ns = xs.shape[0]
    nj = D_FF // tf
    return pl.pallas_call(
        _ffn_kernel,
        grid=(m // tm, nj),
        in_specs=[
            pl.BlockSpec((tm, D_MODEL), lambda i, j: (i, 0)),
            pl.BlockSpec((ns, D_MODEL), lambda i, j: (0, 0)),
            pl.BlockSpec((None, None, 1, D_MODEL), lambda i, j: (layer, gain_idx, 0, 0)),
            pl.BlockSpec((None, None, D_MODEL, tf), lambda i, j: (layer, half, 0, j)),
            pl.BlockSpec((None, None, D_MODEL, tf), lambda i, j: (layer, half, 0, j + nj)),
            pl.BlockSpec((None, None, tf, D_MODEL), lambda i, j: (layer, half, j, 0)),
        ],
        out_specs=[
            pl.BlockSpec((tm, D_MODEL), lambda i, j: (i, 0)),
            pl.BlockSpec((ns, D_MODEL), lambda i, j: (0, 0)),
        ],
        out_shape=[
            jax.ShapeDtypeStruct((m, D_MODEL), F32),
            jax.ShapeDtypeStruct((ns, D_MODEL), F32),
        ],
        scratch_shapes=[
            pltpu.VMEM((tm + BF16_SUBLANES, D_MODEL), BF16),
            pltpu.VMEM((BF16_SUBLANES, D_MODEL), F32),
        ],
        compiler_params=_params("arbitrary", "arbitrary"),
        name="ffn",
    )(x, xs, norm_g, w_in, w_in, w_out)


def _head_norm(y, gain, scale):
    parts = []
    for h in range(DIL_HEADS):
        yh = y[:, h * DIL_HD:(h + 1) * DIL_HD]
        parts.append(yh * _rms_scale(yh) * gain * scale)
    return jnp.concatenate(parts, axis=-1)


def _norm_matmul_kernel(x_ref, g_ref, w_ref, hg_ref, o_ref, xn_ref, *, mode):
    j = pl.program_id(1)

    @pl.when(j == 0)
    def _():
        x = x_ref[...]
        xn_ref[...] = (x * _rms_scale(x) * g_ref[...]).astype(BF16)

    y = _dot(xn_ref[...], w_ref[...].astype(BF16))
    if mode == "plain":
        o_ref[...] = y
    elif mode == "q":
        o_ref[...] = _head_norm(y, hg_ref[pl.ds(j, 1), :], DIL_HD ** -0.5)
    else:
        @pl.when(j % 2 == 0)
        def _():
            o_ref[...] = _head_norm(y, hg_ref[pl.ds(j // 2, 1), :], 1.0)

        @pl.when(j % 2 == 1)
        def _():
            o_ref[...] = y


def _norm_matmul(x, gain_arr, gain_spec, w_arr, w_spec, n, head_gain, mode, tm, tn):
    m = x.shape[0]
    return pl.pallas_call(
        functools.partial(_norm_matmul_kernel, mode=mode),
        grid=(m // tm, n // tn),
        in_specs=[
            pl.BlockSpec((tm, D_MODEL), lambda i, j: (i, 0)),
            gain_spec,
            w_spec,
            pl.BlockSpec(head_gain.shape, lambda i, j: (0, 0)),
        ],
        out_specs=pl.BlockSpec((tm, tn), lambda i, j: (i, j)),
        out_shape=jax.ShapeDtypeStruct((m, n), F32),
        scratch_shapes=[pltpu.VMEM((tm, D_MODEL), BF16)],
        compiler_params=_params("parallel", "arbitrary"),
        name="norm_matmul_" + mode,
    )(x, gain_arr, w_arr, head_gain)


def _kv_streams_kernel(x_ref, g_ref, w_ref, hg_ref, *refs):
    outs = refs[:2 * N_GROUPS]
    y_ref = refs[2 * N_GROUPS]
    tm = x_ref.shape[0]
    x = x_ref[...]
    xn = (x * _rms_scale(x) * g_ref[...]).astype(BF16)
    n_direct = 2 * sum(1 for _, dil in DIL_GROUPS if dil == 1)
    for jj in [2 * g + kv for g in reversed(range(N_GROUPS)) for kv in range(2)]:
        group = jj // 2
        dil = DIL_GROUPS[group][1]
        out = outs[jj]
        y = _dot(xn, w_ref[:, jj * DIL_DIM:(jj + 1) * DIL_DIM])
        val = _head_norm(y, hg_ref[group:group + 1, :], 1.0) if jj % 2 == 0 else y
        if dil == 1:
            out[0] = val
        else:
            slot = jj - n_direct
            for h in range(DIL_HEADS):
                y_ref[slot, h] = val[:, h * DIL_HD:(h + 1) * DIL_HD]
            for r in range(dil):
                for h in range(DIL_HEADS):
                    out[r, :, h * DIL_HD:(h + 1) * DIL_HD] = y_ref[
                        slot, h, pl.ds(r, tm // dil, stride=dil), :]


def _kv_streams(x, gain_arr, w_arr, head_gain, batch, seq, tm):
    nb = seq // tm
    n_out = 2 * N_GROUPS
    out_specs, out_shapes = [], []
    for jj in range(n_out):
        dil = DIL_GROUPS[jj // 2][1]
        out_specs.append(pl.BlockSpec((None, dil, tm // dil, DIL_DIM),
                                      lambda i: (i // nb, 0, i % nb, 0)))
        out_shapes.append(jax.ShapeDtypeStruct((batch, dil, seq // dil, DIL_DIM), F32))
    n_strided = n_out - 2 * sum(1 for _, dil in DIL_GROUPS if dil == 1)
    return pl.pallas_call(
        _kv_streams_kernel,
        grid=(batch * nb,),
        in_specs=[
            pl.BlockSpec((tm, D_MODEL), lambda i: (i, 0)),
            pl.BlockSpec((1, D_MODEL), lambda i: (0, 0)),
            pl.BlockSpec((D_MODEL, n_out * DIL_DIM), lambda i: (0, 0), pipeline_mode=pl.Buffered(1)),
            pl.BlockSpec(head_gain.shape, lambda i: (0, 0)),
        ],
        out_specs=out_specs,
        out_shape=out_shapes,
        scratch_shapes=[pltpu.VMEM((n_strided, DIL_HEADS, tm, DIL_HD), F32)],
        compiler_params=_params("parallel"),
        name="kv_streams",
    )(x, gain_arr, w_arr, head_gain)


def _q_streams_kernel(x_ref, g_ref, w_ref, hg_ref, o0_ref, o1_ref, o2_ref, y_ref):
    tm = x_ref.shape[0]
    x = x_ref[...]
    xn = (x * _rms_scale(x) * g_ref[...]).astype(BF16)
    for group, out in reversed(list(enumerate((o0_ref, o1_ref, o2_ref)))):
        dil = DIL_GROUPS[group][1]
        y = _dot(xn, w_ref[:, group * DIL_DIM:(group + 1) * DIL_DIM].astype(BF16))
        val = _head_norm(y, hg_ref[group:group + 1, :], DIL_HD ** -0.5)
        if dil == 1:
            out[0] = val.astype(out.dtype)
        else:
            for h in range(DIL_HEADS):
                y_ref[group - 1, h] = val[:, h * DIL_HD:(h + 1) * DIL_HD]
            for r in range(dil):
                for h in range(DIL_HEADS):
                    out[r, :, h * DIL_HD:(h + 1) * DIL_HD] = y_ref[
                        group - 1, h, pl.ds(r, tm // dil, stride=dil), :].astype(out.dtype)


def _q_streams(x, gain_arr, gain_spec, w_arr, bi, head_gain, batch, seq, tm):
    nb = seq // tm
    out_specs, out_shapes = [], []
    for _, dil in DIL_GROUPS:
        out_specs.append(pl.BlockSpec((None, dil, tm // dil, DIL_DIM),
                                      lambda i: (i // nb, 0, i % nb, 0)))
        out_shapes.append(jax.ShapeDtypeStruct((batch, dil, seq // dil, DIL_DIM), BF16))
    return pl.pallas_call(
        _q_streams_kernel,
        grid=(batch * nb,),
        in_specs=[
            pl.BlockSpec((tm, D_MODEL), lambda i: (i, 0)),
            gain_spec,
            pl.BlockSpec((None, D_MODEL, N_GROUPS * DIL_DIM), lambda i: (bi, 0, 0),
                         pipeline_mode=pl.Buffered(1)),
            pl.BlockSpec(head_gain.shape, lambda i: (0, 0)),
        ],
        out_specs=out_specs,
        out_shape=out_shapes,
        scratch_shapes=[pltpu.VMEM((N_GROUPS - 1, DIL_HEADS, tm, DIL_HD), F32)],
        compiler_params=_params("parallel"),
        name="q_streams",
    )(x, gain_arr, w_arr, head_gain)


def _matmul_res_kernel(a_ref, w_ref, r_ref, o_ref):
    o_ref[...] = r_ref[...] + _dot(a_ref[...].astype(BF16), w_ref[...].astype(BF16))


def _matmul_res(a, w_arr, layer, res, tm):
    m, k = a.shape
    n = res.shape[1]
    return pl.pallas_call(
        _matmul_res_kernel,
        grid=(m // tm,),
        in_specs=[
            pl.BlockSpec((tm, k), lambda i: (i, 0)),
            pl.BlockSpec((None, k, n), lambda i: (layer, 0, 0), pipeline_mode=pl.Buffered(1)),
            pl.BlockSpec((tm, n), lambda i: (i, 0)),
        ],
        out_specs=pl.BlockSpec((tm, n), lambda i: (i, 0)),
        out_shape=jax.ShapeDtypeStruct((m, n), F32),
        compiler_params=_params("parallel"),
        name="matmul_res",
    )(a, w_arr, res)


def _hgrn_gates(z, lb):
    e = jnp.exp(-jnp.abs(z))
    inv = 1.0 / (1.0 + e)
    pos = z >= 0
    sig = jnp.where(pos, inv, e * inv)
    sig_neg = jnp.where(pos, e * inv, inv)
    k = (1.0 - lb) * sig_neg
    f = lb + (1.0 - lb) * sig
    return f, k, jnp.log(jnp.maximum(f, 1e-30))


def _hgrn_pairwise_chunk(q, k, v, cum):
    c = q.shape[0]
    n_sub = c // 8
    sub_row = lax.broadcasted_iota(jnp.int32, (8, 1), 0)
    acc = [jnp.zeros((8, HG_DK), F32) for _ in range(n_sub)]
    for s in range(c):
        cs = cum[s:s + 1, :]
        ks = k[s:s + 1, :]
        vs = v[s:s + 1, :]
        for tg in range(s // 8, n_sub):
            rows = slice(tg * 8, (tg + 1) * 8)
            p = q[rows, :] * ks * jnp.exp(cum[rows, :] - cs)
            a = jnp.sum(p, axis=-1, keepdims=True)
            if tg == s // 8:
                a = jnp.where(sub_row >= (s - tg * 8), a, 0.0)
            acc[tg] = acc[tg] + a * vs
    return jnp.concatenate(acc, axis=0)


def _hgrn_scan_kernel(q_ref, z_ref, v_ref, gt_ref, lb_ref, go_ref, o_ref, s_ref,
                      st_ref, oi_ref, k_ref, cum_ref):
    c = HG_CHUNK
    tile = HG_TILE
    n_chunks = tile // c
    n_par = HG_HEADS_PER_PROGRAM
    hp = pl.program_id(1)
    lbs = [lb_ref[pl.ds(hp * n_par + p, 1), :] for p in range(n_par)]
    lanes = [slice(p * HG_DK, (p + 1) * HG_DK) for p in range(n_par)]
    heads = range(n_par)
    g_out = go_ref[...]
    blk = HG_CUM_BLOCK
    row = lax.broadcasted_iota(jnp.int32, (blk, blk), 0)
    col = lax.broadcasted_iota(jnp.int32, (blk, blk), 1)
    shift = c.bit_length() - 1
    same_chunk = jnp.right_shift(row, shift) == jnp.right_shift(col, shift)
    tri = jnp.where(jnp.logical_and(same_chunk, col <= row), 1.0, 0.0).astype(BF16)
    crow = lax.broadcasted_iota(jnp.int32, (c, c), 0)
    ccol = lax.broadcasted_iota(jnp.int32, (c, c), 1)
    causal = ccol <= crow
    st_ref[...] = jnp.zeros_like(st_ref)

    def body(ti, carry):
        r0 = pl.multiple_of(ti * tile, tile)
        rows = pl.ds(r0, tile)
        chunks = [slice(ci * c, (ci + 1) * c) for ci in range(n_chunks)]
        q = [q_ref[rows, ln] for ln in lanes]
        vb = [v_ref[rows, ln].astype(BF16) for ln in lanes]
        k, split = [], []
        for p in heads:
            _, kp, lf = _hgrn_gates(z_ref[rows, lanes[p]], lbs[p])
            k.append(kp)
            hi = lf.astype(BF16)
            rem = lf - hi.astype(F32)
            mid = rem.astype(BF16)
            split += [hi, mid, (rem - mid.astype(F32)).astype(BF16)]
        split = jnp.concatenate(split, axis=1)
        cum_all = jnp.concatenate(
            [_dot(tri, split[b0:b0 + blk]) for b0 in range(0, tile, blk)], axis=0)
        cum = [cum_all[:, (3 * p) * HG_DK:(3 * p + 1) * HG_DK]
               + cum_all[:, (3 * p + 1) * HG_DK:(3 * p + 2) * HG_DK]
               + cum_all[:, (3 * p + 2) * HG_DK:(3 * p + 3) * HG_DK] for p in heads]
        lasts = [[cum[p][sl.stop - 1:sl.stop, :] for sl in chunks] for p in heads]
        qdb = [(q[p] * jnp.exp(cum[p])).astype(BF16) for p in heads]
        khb = [(k[p] * jnp.exp(-cum[p])).astype(BF16) for p in heads]
        atts = [[lax.dot_general(qdb[p][sl], khb[p][sl], _NT, preferred_element_type=F32)
                 for sl in chunks] for p in heads]
        atts = [[jnp.where(causal, att, 0.0).astype(BF16) for att in atts[p]] for p in heads]
        intra = [[_dot(att, vb[p][sl]) for att, sl in zip(atts[p], chunks)] for p in heads]
        kdb = [jnp.concatenate([k[p][sl] * jnp.exp(last - cum[p][sl])
                                for sl, last in zip(chunks, lasts[p])], axis=0).astype(BF16)
               for p in heads]
        incs = [[lax.dot_general(vb[p][sl], kdb[p][sl], _TN, preferred_element_type=F32)
                 for sl in chunks] for p in heads]
        sts = [[st_ref[p]] for p in heads]
        for p in heads:
            for inc, last in zip(incs[p], lasts[p]):
                sts[p].append(sts[p][-1] * jnp.exp(last) + inc)
            st_ref[p] = sts[p][-1]
        inter = [jnp.concatenate(
            [lax.dot_general(qdb[p][sl], st.astype(BF16), _NT, preferred_element_type=F32)
             for sl, st in zip(chunks, sts[p])], axis=0) for p in heads]
        for p in heads:
            oi_ref[p] = jnp.concatenate(intra[p], axis=0) + inter[p]

        cum_min = cum[0]
        for p in heads[1:]:
            cum_min = jnp.minimum(cum_min, cum[p])

        @pl.when(jnp.min(cum_min) <= -HG_SAFE_DECAY)
        def _():
            for p in heads:
                k_ref[...] = k[p]
                cum_ref[...] = cum[p]
                oi_ref[p] = inter[p]

                def chunk(ci, carry2, p=p):
                    c0 = pl.multiple_of(ci * c, c)
                    oi_ref[p, pl.ds(c0, c), :] += _hgrn_pairwise_chunk(
                        q_ref[pl.ds(r0 + c0, c), lanes[p]], k_ref[pl.ds(c0, c), :],
                        v_ref[pl.ds(r0 + c0, c), lanes[p]], cum_ref[pl.ds(c0, c), :])
                    return carry2

                lax.fori_loop(0, n_chunks, chunk, 0)

        for p in heads:
            o = oi_ref[p]
            o = o * _rms_scale(o) * g_out * _silu(gt_ref[rows, lanes[p]])
            o_ref[rows, lanes[p]] = o.astype(o_ref.dtype)
        return carry

    lax.fori_loop(0, q_ref.shape[0] // tile, body, 0)
    for p in heads:
        s_ref[p] = st_ref[p].T


def _hgrn_scan(proj, lb, g_out, batch, seq):
    n_par = HG_HEADS_PER_PROGRAM
    n_prog = HG_HEADS // n_par
    width = n_par * HG_DK
    col = lambda k: (lambda b, h: (b, k * n_prog + h))
    return pl.pallas_call(
        _hgrn_scan_kernel,
        grid=(batch, n_prog),
        in_specs=[
            pl.BlockSpec((seq, width), col(0)),
            pl.BlockSpec((seq, width), col(1)),
            pl.BlockSpec((seq, width), col(2)),
            pl.BlockSpec((seq, width), col(3)),
            pl.BlockSpec((HG_HEADS, HG_DK), lambda b, h: (0, 0)),
            pl.BlockSpec((1, HG_DK), lambda b, h: (0, 0)),
        ],
        out_specs=[
            pl.BlockSpec((seq, width), lambda b, h: (b, h)),
            pl.BlockSpec((None, n_par, HG_DK, HG_DK), lambda b, h: (b, h, 0, 0)),
        ],
        out_shape=[
            jax.ShapeDtypeStruct((batch * seq, D_MODEL), BF16),
            jax.ShapeDtypeStruct((batch, HG_HEADS, HG_DK, HG_DK), F32),
        ],
        scratch_shapes=[
            pltpu.VMEM((n_par, HG_DK, HG_DK), F32),
            pltpu.VMEM((n_par, HG_TILE, HG_DK), F32),
            pltpu.VMEM((HG_TILE, HG_DK), F32),
            pltpu.VMEM((HG_TILE, HG_DK), F32),
        ],
        compiler_params=_params("parallel", "parallel"),
        name="hgrn_scan",
    )(proj, proj, proj, proj, lb, g_out)


def _hgrn_step_kernel(x_ref, s0_ref, lb_ref, go_ref, o_ref, s_ref):
    nh = HG_HEADS
    x = x_ref[...]
    q = x[0:nh]
    f, k, _ = _hgrn_gates(x[nh:2 * nh], lb_ref[...])
    r = lax.broadcasted_iota(jnp.int32, (HG_DK, HG_DK), 0)
    c = lax.broadcasted_iota(jnp.int32, (HG_DK, HG_DK), 1)
    eye = (r == c).astype(F32)
    cols = lax.dot_general(eye, jnp.concatenate([q, f, k], axis=0), _NT,
                           preferred_element_type=F32, precision=lax.Precision.HIGHEST)
    g_out = go_ref[...]
    for h in range(nh):
        qc = cols[:, h:h + 1]
        fc = cols[:, nh + h:nh + h + 1]
        kc = cols[:, 2 * nh + h:2 * nh + h + 1]
        v = x[2 * nh + h:2 * nh + h + 1, :]
        s_new = fc * s0_ref[h] + kc * v
        s_ref[h] = s_new
        o = jnp.sum(s_new * qc, axis=0, keepdims=True)
        o = o * _rms_scale(o) * g_out * _silu(x[3 * nh + h:3 * nh + h + 1, :])
        o_ref[:, h * HG_DK:(h + 1) * HG_DK] = o


def _hgrn_step(proj, state, layer, lb, g_out):
    b = proj.shape[0]
    nh = HG_HEADS
    o, s = pl.pallas_call(
        _hgrn_step_kernel,
        grid=(b,),
        in_specs=[
            pl.BlockSpec((None, 4 * nh, HG_DK), lambda i: (i, 0, 0)),
            pl.BlockSpec((None, None, nh, HG_DK, HG_DK), lambda i: (layer, i, 0, 0, 0)),
            pl.BlockSpec((nh, HG_DK), lambda i: (0, 0)),
            pl.BlockSpec((1, HG_DK), lambda i: (0, 0)),
        ],
        out_specs=[
            pl.BlockSpec((None, 1, D_MODEL), lambda i: (i, 0, 0)),
            pl.BlockSpec((None, nh, HG_DK, HG_DK), lambda i: (i, 0, 0, 0)),
        ],
        out_shape=[
            jax.ShapeDtypeStruct((b, 1, D_MODEL), F32),
            jax.ShapeDtypeStruct((b, nh, HG_DK, HG_DK), F32),
        ],
        compiler_params=_params("parallel"),
        name="hgrn_step",
    )(proj.reshape(b, 4 * nh, HG_DK), state, lb, g_out)
    return o.reshape(b, D_MODEL), s


def _attn_prompt_kernel(q_ref, k_ref, v_ref, o_ref, lse_ref, kp_ref, vp_ref):
    ut = pl.program_id(2)

    @pl.when(ut == 0)
    def _():
        kp_ref[...] = jnp.zeros_like(kp_ref)
        vp_ref[...] = jnp.zeros_like(vp_ref)

    qi = lax.broadcasted_iota(jnp.int32, (BAND, BAND), 0)
    kj = lax.broadcasted_iota(jnp.int32, (BAND, BAND), 1)
    steps_c = (qi - kj).astype(F32)
    steps_p = steps_c + float(BAND)
    valid_c = kj <= qi
    valid_p = kj >= qi
    lane = lax.broadcasted_iota(jnp.int32, (BAND, DIL_HD), 1)
    heads = [slice(h * DIL_HD, (h + 1) * DIL_HD) for h in range(DIL_HEADS)]
    n_sub = q_ref.shape[0] // BAND
    subs = [slice(t * BAND, (t + 1) * BAND) for t in range(n_sub)]
    q = q_ref[...]
    kall, vall = k_ref[...].astype(BF16), v_ref[...].astype(BF16)
    kprev = [kp_ref[...]] + [kall[t] for t in subs[:-1]]
    vprev = [vp_ref[...]] + [vall[t] for t in subs[:-1]]
    sc = [[lax.dot_general(q[t, sl], kall[t, sl], _NT, preferred_element_type=F32) for sl in heads]
          for t in subs]
    sp = [[lax.dot_general(q[t, sl], kprev[i][:, sl], _NT, preferred_element_type=F32)
           for sl in heads] for i, t in enumerate(subs)]
    pcs, pps, ms = [], [], []
    for i in range(n_sub):
        ok_p = jnp.logical_and(valid_p, ut > 0) if i == 0 else valid_p
        for h in range(DIL_HEADS):
            slope = 2.0 ** -(h + 1)
            c = jnp.where(valid_c, sc[i][h] - slope * steps_c, NEG)
            p = jnp.where(ok_p, sp[i][h] - slope * steps_p, NEG)
            m = jnp.max(jnp.maximum(c, p), axis=-1, keepdims=True)
            pcs.append(jnp.exp(c - m).astype(BF16))
            pps.append(jnp.exp(p - m).astype(BF16))
            ms.append(m)
    ones = jnp.ones((BAND, DIL_HD), BF16)
    ext = [_dot(pcs[i * DIL_HEADS + h], jnp.concatenate([vall[t, sl], ones], axis=1))
           + _dot(pps[i * DIL_HEADS + h], jnp.concatenate([vprev[i][:, sl], ones], axis=1))
           for i, t in enumerate(subs) for h, sl in enumerate(heads)]
    for i, t in enumerate(subs):
        lse_all = jnp.zeros((BAND, DIL_HD), F32)
        for h, sl in enumerate(heads):
            e = ext[i * DIL_HEADS + h]
            l = e[:, DIL_HD:]
            o_ref[t, sl] = e[:, :DIL_HD] / l
            lse_all = jnp.where(lane == h, ms[i * DIL_HEADS + h] + jnp.log(l), lse_all)
        lse_ref[t, :] = lse_all
    kp_ref[...] = kall[subs[-1]]
    vp_ref[...] = vall[subs[-1]]


def _attn_prompt_group(q, k, v, gi):
    batch, dil, ls, _ = q.shape
    rows = 2 * BAND
    assert ls % rows == 0, (ls, rows)
    tile = pl.BlockSpec((None, None, rows, DIL_DIM), lambda b, r, u: (b, r, u, 0))
    return pl.pallas_call(
        _attn_prompt_kernel,
        grid=(batch, dil, ls // rows),
        in_specs=[tile, tile, tile],
        out_specs=[tile, pl.BlockSpec((None, None, rows, DIL_HD), lambda b, r, u: (b, r, u, 0))],
        out_shape=[
            jax.ShapeDtypeStruct((batch, dil, ls, DIL_DIM), F32),
            jax.ShapeDtypeStruct((batch, dil, ls, DIL_HD), F32),
        ],
        scratch_shapes=[pltpu.VMEM((BAND, DIL_DIM), BF16), pltpu.VMEM((BAND, DIL_DIM), BF16)],
        compiler_params=_params("parallel", "parallel", "arbitrary"),
        name="attn_prompt_g%d" % gi,
    )(q, k, v)


def _merge_matmul_res_kernel(o0, o1, o2, l0, l1, l2, w_ref, r_ref, out_ref, ot_ref, lt_ref):
    tm = out_ref.shape[0]
    sub = MERGE_SUB_ROWS
    wb = w_ref[...].astype(BF16)
    for s0 in range(0, tm, sub):
        rows = slice(s0, s0 + sub)
        for gi, (o_ref, l_ref) in enumerate(((o1, l1), (o2, l2))):
            dil = DIL_GROUPS[gi + 1][1]
            us = slice(s0 // dil, (s0 + sub) // dil)
            for r in range(dil):
                for h in range(DIL_HEADS):
                    ot_ref[gi, h, pl.ds(s0 + r, sub // dil, stride=dil), :] = o_ref[
                        r, us, h * DIL_HD:(h + 1) * DIL_HD]
                lt_ref[gi, pl.ds(s0 + r, sub // dil, stride=dil), :] = l_ref[r, us, :]
        a0, a1, a2 = l0[0, rows, :], lt_ref[0, rows, :], lt_ref[1, rows, :]
        m = jnp.maximum(jnp.maximum(a0, a1), a2)
        e0, e1, e2 = jnp.exp(a0 - m), jnp.exp(a1 - m), jnp.exp(a2 - m)
        inv = 1.0 / (e0 + e1 + e2)
        w0, w1, w2 = e0 * inv, e1 * inv, e2 * inv
        parts = []
        for h in range(DIL_HEADS):
            sl = slice(h * DIL_HD, (h + 1) * DIL_HD)
            parts.append((o0[0, rows, sl] * w0[:, h:h + 1] + ot_ref[0, h, rows, :] * w1[:, h:h + 1]
                          + ot_ref[1, h, rows, :] * w2[:, h:h + 1]).astype(BF16))
        out_ref[rows, :] = r_ref[rows, :] + _dot(jnp.concatenate(parts, axis=1), wb)


def _merge_matmul_res(os, lses, w_arr, layer, res, seq, tm):
    m = res.shape[0]
    nb = seq // tm
    in_specs = []
    for width in (DIL_DIM, DIL_HD):
        for _, dil in DIL_GROUPS:
            in_specs.append(pl.BlockSpec((None, dil, tm // dil, width),
                                         lambda i: (i // nb, 0, i % nb, 0)))
    in_specs += [
        pl.BlockSpec((None, DIL_DIM, D_MODEL), lambda i: (layer, 0, 0), pipeline_mode=pl.Buffered(1)),
        pl.BlockSpec((tm, D_MODEL), lambda i: (i, 0)),
    ]
    return pl.pallas_call(
        _merge_matmul_res_kernel,
        grid=(m // tm,),
        in_specs=in_specs,
        out_specs=pl.BlockSpec((tm, D_MODEL), lambda i: (i, 0)),
        out_shape=jax.ShapeDtypeStruct((m, D_MODEL), F32),
        scratch_shapes=[
            pltpu.VMEM((N_GROUPS - 1, DIL_HEADS, tm, DIL_HD), F32),
            pltpu.VMEM((N_GROUPS - 1, tm, DIL_HD), F32),
        ],
        compiler_params=_params("parallel"),
        name="merge_matmul_res",
    )(*os, *lses, w_arr, res)


def _attn_sample_kernel(q_ref, kvn_ref, c0_ref, c1_ref, c2_ref, o_ref):
    caches = (c0_ref, c1_ref, c2_ref)
    steps = float(BAND) - lax.broadcasted_iota(jnp.int32, (BAND, 1, 1), 0).astype(F32)
    head = lax.broadcasted_iota(jnp.int32, (1, DIL_HEADS, 1), 1)
    slope = jnp.zeros((1, DIL_HEADS, 1), F32)
    for h in range(DIL_HEADS):
        slope = jnp.where(head == h, 2.0 ** -(h + 1), slope)
    bias = slope * steps
    outs, lses = [], []
    for g in range(N_GROUPS):
        c_ref = caches[g]
        q = q_ref[g]
        kn, vn = kvn_ref[g, 0], kvn_ref[g, 1]
        sb = jnp.sum(c_ref[:, 0] * q[None], axis=-1, keepdims=True) - bias
        sn = jnp.sum(kn * q, axis=-1, keepdims=True)
        m = jnp.maximum(jnp.max(sb, axis=0), sn)
        pb = jnp.exp(sb - m[None])
        pn = jnp.exp(sn - m)
        l = jnp.sum(pb, axis=0) + pn
        outs.append((jnp.sum(pb * c_ref[:, 1], axis=0) + pn * vn) / l)
        lses.append(m + jnp.log(l))
    m = jnp.maximum(jnp.maximum(lses[0], lses[1]), lses[2])
    es = [jnp.exp(a - m) for a in lses]
    inv = 1.0 / (es[0] + es[1] + es[2])
    o_ref[...] = outs[0] * (es[0] * inv) + outs[1] * (es[1] * inv) + outs[2] * (es[2] * inv)


def _attn_sample(q, kv_new, caches):
    b = q.shape[0]
    views, cache_specs = [], []
    for cache, (_, dil) in zip(caches, DIL_GROUPS):
        wb = cache.shape[1]
        views.append(cache.reshape(b, wb // dil, dil, 2, DIL_HEADS, DIL_HD))
        cache_specs.append(pl.BlockSpec((None, BAND, None, 2, DIL_HEADS, DIL_HD),
                                        lambda i: (i, 0, 0, 0, 0, 0)))
    o = pl.pallas_call(
        _attn_sample_kernel,
        grid=(b,),
        in_specs=[
            pl.BlockSpec((None, N_GROUPS, DIL_HEADS, DIL_HD), lambda i: (i, 0, 0, 0)),
            pl.BlockSpec((None, N_GROUPS, 2, DIL_HEADS, DIL_HD), lambda i: (i, 0, 0, 0, 0)),
        ] + cache_specs,
        out_specs=pl.BlockSpec((None, DIL_HEADS, DIL_HD), lambda i: (i, 0, 0)),
        out_shape=jax.ShapeDtypeStruct((b, DIL_HEADS, DIL_HD), F32),
        compiler_params=_params("parallel"),
        name="attn_sample",
    )(q.reshape(b, N_GROUPS, DIL_HEADS, DIL_HD),
      kv_new.reshape(b, N_GROUPS, 2, DIL_HEADS, DIL_HD), *views)
    return o.reshape(b, DIL_DIM)


def _trunk(x, xs, batch, seq, hg_state, caches, wts):
    (norm_g, ffn_w_in, ffn_w_out, hg_w_in, hg_w_out, lbs, hg_out_g, kv_norm_g, w_kv,
     k_norm_g, dil_w_q, dil_q_norm_g, dil_w_o) = wts
    ns = xs.shape[0]
    tm, tn = 512, 1024
    states_p, states_s = [], []
    kv_p = kv_s = None
    gain = lambda layer, idx: pl.BlockSpec((None, None, 1, D_MODEL), lambda i, j: (layer, idx, 0, 0))
    for layer in range(DEPTH):
        x, xs = _ffn(x, xs, norm_g, ffn_w_in, ffn_w_out, layer, 0, 0)
        if layer < N_A_LAYERS:
            w_spec = pl.BlockSpec((None, D_MODEL, tn), lambda i, j, layer=layer: (layer, 0, j))
            lb = lbs[layer].reshape(HG_HEADS, HG_DK)
            g_out = hg_out_g[layer].reshape(1, HG_DK)
            wide_spec = pl.BlockSpec((None, D_MODEL, 2 * tn), lambda i, j, layer=layer: (layer, 0, j))
            proj = _norm_matmul(x, norm_g, gain(layer, 1), hg_w_in, wide_spec,
                                4 * D_MODEL, k_norm_g, "plain", 2 * tm, 2 * tn)
            o, s = _hgrn_scan(proj, lb, g_out, batch, seq)
            states_p.append(s)
            x = _matmul_res(o, hg_w_out, layer, x, tm)
            proj = _norm_matmul(xs, norm_g, gain(layer, 1), hg_w_in, w_spec,
                                4 * D_MODEL, k_norm_g, "plain", ns, tn)
            o, s = _hgrn_step(proj, hg_state, layer, lb, g_out)
            states_s.append(s)
            xs = _matmul_res(o, hg_w_out, layer, xs, ns)
        else:
            bi = layer - N_A_LAYERS
            qs = _q_streams(x, norm_g,
                            pl.BlockSpec((None, None, 1, D_MODEL), lambda i, layer=layer: (layer, 1, 0, 0)),
                            dil_w_q, bi, dil_q_norm_g[bi], batch, seq, tm)
            parts = [_attn_prompt_group(qs[gi], kv_p[2 * gi], kv_p[2 * gi + 1], gi)
                     for gi in range(N_GROUPS)]
            x = _merge_matmul_res([p[0] for p in parts], [p[1] for p in parts],
                                  dil_w_o, bi, x, seq, tm)
            w_spec = pl.BlockSpec((None, D_MODEL, DIL_DIM), lambda i, j, bi=bi: (bi, 0, j))
            q = _norm_matmul(xs, norm_g, gain(layer, 1), dil_w_q, w_spec,
                             N_GROUPS * DIL_DIM, dil_q_norm_g[bi], "q", ns, DIL_DIM)
            xs = _matmul_res(_attn_sample(q, kv_s, caches), dil_w_o, bi, xs, ns)
        x, xs = _ffn(x, xs, norm_g, ffn_w_in, ffn_w_out, layer, 1, 2)
        if layer == N_A_LAYERS - 1:
            kv_gain = kv_norm_g.reshape(1, D_MODEL)
            kv_gain_spec = pl.BlockSpec((1, D_MODEL), lambda i, j: (0, 0))
            w_spec = pl.BlockSpec((D_MODEL, DIL_DIM), lambda i, j: (0, j))
            kv_p = _kv_streams(x, kv_gain, w_kv, k_norm_g, batch, seq, tm // 2)
            kv_s = _norm_matmul(xs, kv_gain, kv_gain_spec, w_kv, w_spec,
                                2 * N_GROUPS * DIL_DIM, k_norm_g, "kv", ns, DIL_DIM)
    return x, xs, jnp.stack(states_p, axis=0), jnp.stack(states_s, axis=0), kv_p, kv_s


def kernel(x_prompt, x_sample, state_hgrn, cache_kv_w128, cache_kv_w512, cache_kv_w2048,
           norm_g, ffn_w_in, ffn_w_out, hg_w_in, hg_w_out, hg_lb_logits, hg_out_g,
           kv_norm_g, w_kv, k_norm_g, dil_w_q, dil_q_norm_g, dil_w_o):
    batch, seq, _ = x_prompt.shape
    dec_batch = x_sample.shape[0]
    p = jax.nn.softmax(hg_lb_logits.astype(F32), axis=0)
    lbs = jnp.cumsum(p, axis=0) - p[0]
    wts = (norm_g.reshape(DEPTH, 3, 1, D_MODEL), ffn_w_in, ffn_w_out,
           hg_w_in.astype(BF16), hg_w_out, lbs, hg_out_g, kv_norm_g,
           w_kv.astype(BF16), k_norm_g, dil_w_q, dil_q_norm_g, dil_w_o)

    y_p, y_s, hg_p, hg_s, kv_p, kv_s = _trunk(
        x_prompt.reshape(batch * seq, D_MODEL), x_sample.reshape(dec_batch, D_MODEL), batch, seq,
        state_hgrn, (cache_kv_w128, cache_kv_w512, cache_kv_w2048), wts)

    kv_s = kv_s.reshape(dec_batch, 1, N_GROUPS, 2, DIL_HEADS, DIL_HD)
    outs = [y_p.reshape(batch, seq, D_MODEL), y_s.reshape(dec_batch, 1, D_MODEL), hg_p, hg_s]
    for gi, (win, dil) in enumerate(DIL_GROUPS):
        rows = min(win, seq)
        tail = [jnp.swapaxes(a[:, :, (seq - rows) // dil:], 1, 2).reshape(
            batch, rows, DIL_HEADS, DIL_HD) for a in (kv_p[2 * gi], kv_p[2 * gi + 1])]
        outs.append(jnp.stack(tail, axis=2))
        outs.append(kv_s[:, :, gi])
    return tuple(outs)
```

```python
import functools

import jax
import jax.numpy as jnp
from jax import lax
from jax.experimental import pallas as pl
from jax.experimental.pallas import tpu as pltpu

D_MODEL = 2048
D_FF = 5632
DEPTH = 4
N_A_LAYERS = 2
HG_HEADS = 16
HG_DK = 128
DIL_GROUPS = ((128, 1), (512, 4), (2048, 16))
N_GROUPS = 3
DIL_HEADS = 8
DIL_HD = 128
DIL_DIM = DIL_HEADS * DIL_HD
BAND = 128
EPS = 1e-6
NEG = -1e30
F32 = jnp.float32
BF16 = jnp.bfloat16

VMEM_LIMIT_BYTES = 56 * 1024 * 1024
BF16_SUBLANES = 16
LANES = 128
HG_CHUNK = 32
HG_TILE = 512
HG_HEADS_PER_PROGRAM = 2
HG_CUM_BLOCK = 256
MERGE_SUB_ROWS = 256
HG_SAFE_DECAY = 80.0

_NT = (((1,), (1,)), ((), ()))
_TN = (((0,), (0,)), ((), ()))


def _params(*semantics):
    return pltpu.CompilerParams(dimension_semantics=semantics, vmem_limit_bytes=VMEM_LIMIT_BYTES)


def _rms_scale(x):
    return lax.rsqrt(jnp.mean(x * x, axis=-1, keepdims=True) + EPS)


def _silu(x):
    return x / (1.0 + jnp.exp(-x))


def _dot(a, b):
    return jnp.dot(a, b, preferred_element_type=F32)


def _ffn_kernel(x_ref, xs_ref, g_ref, wg_ref, wu_ref, wo_ref, *rest, n_cast):
    if n_cast:
        cast_in_ref, o_ref, os_ref, cast_out_ref, xn_ref, accs_ref = rest
    else:
        o_ref, os_ref, xn_ref, accs_ref = rest
    j = pl.program_id(1)
    tm = x_ref.shape[0]
    ns = xs_ref.shape[0]

    if n_cast:
        @pl.when(pl.program_id(0) * pl.num_programs(1) + j < n_cast)
        def _():
            cast_out_ref[...] = cast_in_ref[...].astype(BF16)

    @pl.when(j == 0)
    def _():
        g = g_ref[...]
        x = x_ref[...]
        xn_ref[0:tm, :] = (x * _rms_scale(x) * g).astype(BF16)
        xs = xs_ref[...]
        xsn = jnp.concatenate(
            [xs * _rms_scale(xs) * g, jnp.zeros((BF16_SUBLANES - ns, D_MODEL), F32)], axis=0)
        xn_ref[tm:tm + BF16_SUBLANES, :] = xsn.astype(BF16)
        o_ref[...] = jnp.zeros_like(o_ref)
        accs_ref[...] = jnp.zeros_like(accs_ref)

    def step(rows):
        xn = xn_ref[0:rows, :]
        gate = _dot(xn, wg_ref[...].astype(BF16))
        up = _dot(xn, wu_ref[...].astype(BF16))
        act = (_silu(gate) * up).astype(BF16)
        return _dot(act, wo_ref[...].astype(BF16))

    first_tile = pl.program_id(0) == 0

    @pl.when(first_tile)
    def _():
        res = step(tm + BF16_SUBLANES)
        o_ref[...] += res[0:tm]
        accs_ref[...] += res[tm:tm + BF16_SUBLANES]

    @pl.when(jnp.logical_not(first_tile))
    def _():
        o_ref[...] += step(tm)

    last = j == pl.num_programs(1) - 1

    @pl.when(last)
    def _():
        o_ref[...] = x_ref[...] + 0.5 * o_ref[...]

    @pl.when(jnp.logical_and(last, first_tile))
    def _():
        os_ref[...] = xs_ref[...] + 0.5 * accs_ref[0:ns, :]


def _ffn(x, xs, norm_g, w_in, w_out, layer, half, gain_idx, cast=None, tm=1024, tf=256):
    m = x.shape[0]
    ns = xs.shape[0]
    nj = D_FF // tf
    in_specs = [
        pl.BlockSpec((tm, D_MODEL), lambda i, j: (i, 0)),
        pl.BlockSpec((ns, D_MODEL), lambda i, j: (0, 0)),
        pl.BlockSpec((None, None, 1, D_MODEL), lambda i, j: (layer, gain_idx, 0, 0)),
        pl.BlockSpec((None, None, D_MODEL, tf), lambda i, j: (layer, half, 0, j)),
        pl.BlockSpec((None, None, D_MODEL, tf), lambda i, j: (layer, half, 0, j + nj)),
        pl.BlockSpec((None, None, tf, D_MODEL), lambda i, j: (layer, half, j, 0)),
    ]
    out_specs = [
        pl.BlockSpec((tm, D_MODEL), lambda i, j: (i, 0)),
        pl.BlockSpec((ns, D_MODEL), lambda i, j: (0, 0)),
    ]
    out_shape = [
        jax.ShapeDtypeStruct((m, D_MODEL), F32),
        jax.ShapeDtypeStruct((ns, D_MODEL), F32),
    ]
    args = [x, xs, norm_g, w_in, w_in, w_out]
    n_cast = 0
    if cast is not None:
        cast_w, lead = cast
        n_cols = cast_w.shape[-1]
        n_cast = n_cols // LANES
        assert n_cast <= (m // tm) * nj, (n_cast, m // tm, nj)
        blk = lambda i, j: jnp.minimum(i * nj + j, n_cast - 1)
        in_specs.append(pl.BlockSpec((None,) * len(lead) + (D_MODEL, LANES),
                                     lambda i, j: lead + (0, blk(i, j))))
        out_specs.append(pl.BlockSpec((D_MODEL, LANES), lambda i, j: (0, blk(i, j))))
        out_shape.append(jax.ShapeDtypeStruct((D_MODEL, n_cols), BF16))
        args.append(cast_w)
    return pl.pallas_call(
        functools.partial(_ffn_kernel, n_cast=n_cast),
        grid=(m // tm, nj),
        in_specs=in_specs,
        out_specs=out_specs,
        out_shape=out_shape,
        scratch_shapes=[
            pltpu.VMEM((tm + BF16_SUBLANES, D_MODEL), BF16),
            pltpu.VMEM((BF16_SUBLANES, D_MODEL), F32),
        ],
        compiler_params=_params("arbitrary", "arbitrary"),
        name="ffn",
    )(*args)


def _head_norm(y, gain, scale):
    parts = []
    for h in range(DIL_HEADS):
        yh = y[:, h * DIL_HD:(h + 1) * DIL_HD]
        parts.append(yh * _rms_scale(yh) * gain * scale)
    return jnp.concatenate(parts, axis=-1)


def _norm_matmul_kernel(x_ref, g_ref, w_ref, hg_ref, o_ref, xn_ref, *, mode):
    j = pl.program_id(1)

    @pl.when(j == 0)
    def _():
        x = x_ref[...]
        xn_ref[...] = (x * _rms_scale(x) * g_ref[...]).astype(BF16)

    y = _dot(xn_ref[...], w_ref[...].astype(BF16))
    if mode == "plain":
        o_ref[...] = y
    elif mode == "q":
        o_ref[...] = _head_norm(y, hg_ref[pl.ds(j, 1), :], DIL_HD ** -0.5)
    else:
        @pl.when(j % 2 == 0)
        def _():
            o_ref[...] = _head_norm(y, hg_ref[pl.ds(j // 2, 1), :], 1.0)

        @pl.when(j % 2 == 1)
        def _():
            o_ref[...] = y


def _norm_matmul(x, gain_arr, gain_spec, w_arr, w_spec, n, head_gain, mode, tm, tn):
    m = x.shape[0]
    return pl.pallas_call(
        functools.partial(_norm_matmul_kernel, mode=mode),
        grid=(m // tm, n // tn),
        in_specs=[
            pl.BlockSpec((tm, D_MODEL), lambda i, j: (i, 0)),
            gain_spec,
            w_spec,
            pl.BlockSpec(head_gain.shape, lambda i, j: (0, 0)),
        ],
        out_specs=pl.BlockSpec((tm, tn), lambda i, j: (i, j)),
        out_shape=jax.ShapeDtypeStruct((m, n), F32),
        scratch_shapes=[pltpu.VMEM((tm, D_MODEL), BF16)],
        compiler_params=_params("parallel", "arbitrary"),
        name="norm_matmul_" + mode,
    )(x, gain_arr, w_arr, head_gain)


def _store_stream_major(y, gain, scale, out, col0, y_ref, dil):
    tm = y.shape[0]
    for h in range(DIL_HEADS):
        piece = y[:, h * DIL_HD:(h + 1) * DIL_HD]
        if gain is not None:
            piece = piece * _rms_scale(piece) * gain * scale
        if dil == 1:
            out[0, :, col0 + h * DIL_HD:col0 + (h + 1) * DIL_HD] = piece.astype(out.dtype)
        else:
            y_ref[h] = piece
    if dil > 1:
        for r in range(dil):
            for h in range(DIL_HEADS):
                out[r, :, col0 + h * DIL_HD:col0 + (h + 1) * DIL_HD] = y_ref[
                    h, pl.ds(r, tm // dil, stride=dil), :].astype(out.dtype)


def _kv_streams_kernel(x_ref, g_ref, w_ref, hg_ref, o0_ref, o1_ref, o2_ref, y_ref):
    outs = (o0_ref, o1_ref, o2_ref)
    x = x_ref[...]
    xn = (x * _rms_scale(x) * g_ref[...]).astype(BF16)
    for jj in [2 * g + kv for g in reversed(range(N_GROUPS)) for kv in range(2)]:
        group, kv = jj // 2, jj % 2
        y = _dot(xn, w_ref[:, jj * DIL_DIM:(jj + 1) * DIL_DIM])
        gain = hg_ref[group:group + 1, :] if kv == 0 else None
        _store_stream_major(y, gain, 1.0, outs[group], kv * DIL_DIM, y_ref.at[jj],
                            DIL_GROUPS[group][1])


def _kv_streams(x, gain_arr, w_arr, head_gain, batch, seq, tm):
    nb = seq // tm
    out_specs, out_shapes = [], []
    for _, dil in DIL_GROUPS:
        out_specs.append(pl.BlockSpec((None, dil, tm // dil, 2 * DIL_DIM),
                                      lambda i: (i // nb, 0, i % nb, 0)))
        out_shapes.append(jax.ShapeDtypeStruct((batch, dil, seq // dil, 2 * DIL_DIM), F32))
    return pl.pallas_call(
        _kv_streams_kernel,
        grid=(batch * nb,),
        in_specs=[
            pl.BlockSpec((tm, D_MODEL), lambda i: (i, 0)),
            pl.BlockSpec((1, D_MODEL), lambda i: (0, 0)),
            pl.BlockSpec((D_MODEL, 2 * N_GROUPS * DIL_DIM), lambda i: (0, 0),
                         pipeline_mode=pl.Buffered(1)),
            pl.BlockSpec(head_gain.shape, lambda i: (0, 0)),
        ],
        out_specs=out_specs,
        out_shape=out_shapes,
        scratch_shapes=[pltpu.VMEM((2 * N_GROUPS, DIL_HEADS, tm, DIL_HD), F32)],
        compiler_params=_params("parallel"),
        name="kv_streams",
    )(x, gain_arr, w_arr, head_gain)


def _q_streams_kernel(x_ref, g_ref, w_ref, hg_ref, o0_ref, o1_ref, o2_ref, y_ref):
    x = x_ref[...]
    xn = (x * _rms_scale(x) * g_ref[...]).astype(BF16)
    for group, out in reversed(list(enumerate((o0_ref, o1_ref, o2_ref)))):
        y = _dot(xn, w_ref[:, group * DIL_DIM:(group + 1) * DIL_DIM].astype(BF16))
        _store_stream_major(y, hg_ref[group:group + 1, :], DIL_HD ** -0.5, out, 0,
                            y_ref.at[group], DIL_GROUPS[group][1])


def _q_streams(x, gain_arr, gain_spec, w_arr, bi, head_gain, batch, seq, tm):
    nb = seq // tm
    out_specs, out_shapes = [], []
    for _, dil in DIL_GROUPS:
        out_specs.append(pl.BlockSpec((None, dil, tm // dil, DIL_DIM),
                                      lambda i: (i // nb, 0, i % nb, 0)))
        out_shapes.append(jax.ShapeDtypeStruct((batch, dil, seq // dil, DIL_DIM), BF16))
    return pl.pallas_call(
        _q_streams_kernel,
        grid=(batch * nb,),
        in_specs=[
            pl.BlockSpec((tm, D_MODEL), lambda i: (i, 0)),
            gain_spec,
            pl.BlockSpec((None, D_MODEL, N_GROUPS * DIL_DIM), lambda i: (bi, 0, 0),
                         pipeline_mode=pl.Buffered(1)),
            pl.BlockSpec(head_gain.shape, lambda i: (0, 0)),
        ],
        out_specs=out_specs,
        out_shape=out_shapes,
        scratch_shapes=[pltpu.VMEM((N_GROUPS, DIL_HEADS, tm, DIL_HD), F32)],
        compiler_params=_params("parallel"),
        name="q_streams",
    )(x, gain_arr, w_arr, head_gain)


def _matmul_res_kernel(a_ref, w_ref, r_ref, o_ref):
    o_ref[...] = r_ref[...] + _dot(a_ref[...].astype(BF16), w_ref[...].astype(BF16))


def _matmul_res(a, w_arr, layer, res, tm):
    m, k = a.shape
    n = res.shape[1]
    return pl.pallas_call(
        _matmul_res_kernel,
        grid=(m // tm,),
        in_specs=[
            pl.BlockSpec((tm, k), lambda i: (i, 0)),
            pl.BlockSpec((None, k, n), lambda i: (layer, 0, 0), pipeline_mode=pl.Buffered(1)),
            pl.BlockSpec((tm, n), lambda i: (i, 0)),
        ],
        out_specs=pl.BlockSpec((tm, n), lambda i: (i, 0)),
        out_shape=jax.ShapeDtypeStruct((m, n), F32),
        compiler_params=_params("parallel"),
        name="matmul_res",
    )(a, w_arr, res)


def _hgrn_gates(z, lb):
    e = jnp.exp(-jnp.abs(z))
    inv = 1.0 / (1.0 + e)
    pos = z >= 0
    sig = jnp.where(pos, inv, e * inv)
    sig_neg = jnp.where(pos, e * inv, inv)
    k = (1.0 - lb) * sig_neg
    f = lb + (1.0 - lb) * sig
    return f, k, jnp.log(jnp.maximum(f, 1e-30))


def _hgrn_pairwise_chunk(q, k, v, cum):
    c = q.shape[0]
    n_sub = c // 8
    sub_row = lax.broadcasted_iota(jnp.int32, (8, 1), 0)
    acc = [jnp.zeros((8, HG_DK), F32) for _ in range(n_sub)]
    for s in range(c):
        cs = cum[s:s + 1, :]
        ks = k[s:s + 1, :]
        vs = v[s:s + 1, :]
        for tg in range(s // 8, n_sub):
            rows = slice(tg * 8, (tg + 1) * 8)
            p = q[rows, :] * ks * jnp.exp(cum[rows, :] - cs)
            a = jnp.sum(p, axis=-1, keepdims=True)
            if tg == s // 8:
                a = jnp.where(sub_row >= (s - tg * 8), a, 0.0)
            acc[tg] = acc[tg] + a * vs
    return jnp.concatenate(acc, axis=0)


def _hgrn_scan_kernel(q_ref, z_ref, v_ref, gt_ref, lb_ref, go_ref, o_ref, s_ref,
                      st_ref, oi_ref, k_ref, cum_ref):
    c = HG_CHUNK
    tile = HG_TILE
    n_chunks = tile // c
    n_par = HG_HEADS_PER_PROGRAM
    hp = pl.program_id(1)
    lbs = [lb_ref[pl.ds(hp * n_par + p, 1), :] for p in range(n_par)]
    lanes = [slice(p * HG_DK, (p + 1) * HG_DK) for p in range(n_par)]
    heads = range(n_par)
    g_out = go_ref[...]
    blk = HG_CUM_BLOCK
    row = lax.broadcasted_iota(jnp.int32, (blk, blk), 0)
    col = lax.broadcasted_iota(jnp.int32, (blk, blk), 1)
    shift = c.bit_length() - 1
    same_chunk = jnp.right_shift(row, shift) == jnp.right_shift(col, shift)
    tri = jnp.where(jnp.logical_and(same_chunk, col <= row), 1.0, 0.0).astype(BF16)
    crow = lax.broadcasted_iota(jnp.int32, (c, c), 0)
    ccol = lax.broadcasted_iota(jnp.int32, (c, c), 1)
    causal = ccol <= crow
    st_ref[...] = jnp.zeros_like(st_ref)

    def body(ti, carry):
        r0 = pl.multiple_of(ti * tile, tile)
        rows = pl.ds(r0, tile)
        chunks = [slice(ci * c, (ci + 1) * c) for ci in range(n_chunks)]
        q = [q_ref[rows, ln] for ln in lanes]
        vb = [v_ref[rows, ln].astype(BF16) for ln in lanes]
        k, split = [], []
        for p in heads:
            _, kp, lf = _hgrn_gates(z_ref[rows, lanes[p]], lbs[p])
            k.append(kp)
            hi = lf.astype(BF16)
            rem = lf - hi.astype(F32)
            mid = rem.astype(BF16)
            split += [hi, mid, (rem - mid.astype(F32)).astype(BF16)]
        split = jnp.concatenate(split, axis=1)
        cum_all = jnp.concatenate(
            [_dot(tri, split[b0:b0 + blk]) for b0 in range(0, tile, blk)], axis=0)
        cum = [cum_all[:, (3 * p) * HG_DK:(3 * p + 1) * HG_DK]
               + cum_all[:, (3 * p + 1) * HG_DK:(3 * p + 2) * HG_DK]
               + cum_all[:, (3 * p + 2) * HG_DK:(3 * p + 3) * HG_DK] for p in heads]
        lasts = [[cum[p][sl.stop - 1:sl.stop, :] for sl in chunks] for p in heads]
        qdb = [(q[p] * jnp.exp(cum[p])).astype(BF16) for p in heads]
        khb = [(k[p] * jnp.exp(-cum[p])).astype(BF16) for p in heads]
        atts = [[lax.dot_general(qdb[p][sl], khb[p][sl], _NT, preferred_element_type=F32)
                 for sl in chunks] for p in heads]
        atts = [[jnp.where(causal, att, 0.0).astype(BF16) for att in atts[p]] for p in heads]
        intra = [[_dot(att, vb[p][sl]) for att, sl in zip(atts[p], chunks)] for p in heads]
        kdb = [jnp.concatenate([k[p][sl] * jnp.exp(last - cum[p][sl])
                                for sl, last in zip(chunks, lasts[p])], axis=0).astype(BF16)
               for p in heads]
        incs = [[lax.dot_general(vb[p][sl], kdb[p][sl], _TN, preferred_element_type=F32)
                 for sl in chunks] for p in heads]
        sts = [[st_ref[p]] for p in heads]
        for p in heads:
            for inc, last in zip(incs[p], lasts[p]):
                sts[p].append(sts[p][-1] * jnp.exp(last) + inc)
            st_ref[p] = sts[p][-1]
        inter = [jnp.concatenate(
            [lax.dot_general(qdb[p][sl], st.astype(BF16), _NT, preferred_element_type=F32)
             for sl, st in zip(chunks, sts[p])], axis=0) for p in heads]
        for p in heads:
            oi_ref[p] = jnp.concatenate(intra[p], axis=0) + inter[p]

        cum_min = cum[0]
        for p in heads[1:]:
            cum_min = jnp.minimum(cum_min, cum[p])

        @pl.when(jnp.min(cum_min) <= -HG_SAFE_DECAY)
        def _():
            for p in heads:
                k_ref[...] = k[p]
                cum_ref[...] = cum[p]
                oi_ref[p] = inter[p]

                def chunk(ci, carry2, p=p):
                    c0 = pl.multiple_of(ci * c, c)
                    oi_ref[p, pl.ds(c0, c), :] += _hgrn_pairwise_chunk(
                        q_ref[pl.ds(r0 + c0, c), lanes[p]], k_ref[pl.ds(c0, c), :],
                        v_ref[pl.ds(r0 + c0, c), lanes[p]], cum_ref[pl.ds(c0, c), :])
                    return carry2

                lax.fori_loop(0, n_chunks, chunk, 0)

        for p in heads:
            o = oi_ref[p]
            o = o * _rms_scale(o) * g_out * _silu(gt_ref[rows, lanes[p]])
            o_ref[rows, lanes[p]] = o.astype(o_ref.dtype)
        return carry

    lax.fori_loop(0, q_ref.shape[0] // tile, body, 0)
    for p in heads:
        s_ref[p] = st_ref[p].T


def _hgrn_scan(proj, lb, g_out, batch, seq):
    n_par = HG_HEADS_PER_PROGRAM
    n_prog = HG_HEADS // n_par
    width = n_par * HG_DK
    col = lambda k: (lambda b, h: (b, k * n_prog + h))
    return pl.pallas_call(
        _hgrn_scan_kernel,
        grid=(batch, n_prog),
        in_specs=[
            pl.BlockSpec((seq, width), col(0)),
            pl.BlockSpec((seq, width), col(1)),
            pl.BlockSpec((seq, width), col(2)),
            pl.BlockSpec((seq, width), col(3)),
            pl.BlockSpec((HG_HEADS, HG_DK), lambda b, h: (0, 0)),
            pl.BlockSpec((1, HG_DK), lambda b, h: (0, 0)),
        ],
        out_specs=[
            pl.BlockSpec((seq, width), lambda b, h: (b, h)),
            pl.BlockSpec((None, n_par, HG_DK, HG_DK), lambda b, h: (b, h, 0, 0)),
        ],
        out_shape=[
            jax.ShapeDtypeStruct((batch * seq, D_MODEL), BF16),
            jax.ShapeDtypeStruct((batch, HG_HEADS, HG_DK, HG_DK), F32),
        ],
        scratch_shapes=[
            pltpu.VMEM((n_par, HG_DK, HG_DK), F32),
            pltpu.VMEM((n_par, HG_TILE, HG_DK), F32),
            pltpu.VMEM((HG_TILE, HG_DK), F32),
            pltpu.VMEM((HG_TILE, HG_DK), F32),
        ],
        compiler_params=_params("parallel", "parallel"),
        name="hgrn_scan",
    )(proj, proj, proj, proj, lb, g_out)


def _hgrn_step_kernel(x_ref, s0_ref, lb_ref, go_ref, o_ref, s_ref):
    nh = HG_HEADS
    x = x_ref[...]
    q = x[0:nh]
    f, k, _ = _hgrn_gates(x[nh:2 * nh], lb_ref[...])
    r = lax.broadcasted_iota(jnp.int32, (HG_DK, HG_DK), 0)
    c = lax.broadcasted_iota(jnp.int32, (HG_DK, HG_DK), 1)
    eye = (r == c).astype(F32)
    cols = lax.dot_general(eye, jnp.concatenate([q, f, k], axis=0), _NT,
                           preferred_element_type=F32, precision=lax.Precision.HIGHEST)
    g_out = go_ref[...]
    for h in range(nh):
        qc = cols[:, h:h + 1]
        fc = cols[:, nh + h:nh + h + 1]
        kc = cols[:, 2 * nh + h:2 * nh + h + 1]
        v = x[2 * nh + h:2 * nh + h + 1, :]
        s_new = fc * s0_ref[h] + kc * v
        s_ref[h] = s_new
        o = jnp.sum(s_new * qc, axis=0, keepdims=True)
        o = o * _rms_scale(o) * g_out * _silu(x[3 * nh + h:3 * nh + h + 1, :])
        o_ref[:, h * HG_DK:(h + 1) * HG_DK] = o


def _hgrn_step(proj, state, layer, lb, g_out):
    b = proj.shape[0]
    nh = HG_HEADS
    o, s = pl.pallas_call(
        _hgrn_step_kernel,
        grid=(b,),
        in_specs=[
            pl.BlockSpec((None, 4 * nh, HG_DK), lambda i: (i, 0, 0)),
            pl.BlockSpec((None, None, nh, HG_DK, HG_DK), lambda i: (layer, i, 0, 0, 0)),
            pl.BlockSpec((nh, HG_DK), lambda i: (0, 0)),
            pl.BlockSpec((1, HG_DK), lambda i: (0, 0)),
        ],
        out_specs=[
            pl.BlockSpec((None, 1, D_MODEL), lambda i: (i, 0, 0)),
            pl.BlockSpec((None, nh, HG_DK, HG_DK), lambda i: (i, 0, 0, 0)),
        ],
        out_shape=[
            jax.ShapeDtypeStruct((b, 1, D_MODEL), F32),
            jax.ShapeDtypeStruct((b, nh, HG_DK, HG_DK), F32),
        ],
        compiler_params=_params("parallel"),
        name="hgrn_step",
    )(proj.reshape(b, 4 * nh, HG_DK), state, lb, g_out)
    return o.reshape(b, D_MODEL), s


def _attn_prompt_kernel(q_ref, k_ref, v_ref, o_ref, lse_ref, kp_ref, vp_ref):
    ut = pl.program_id(2)

    @pl.when(ut == 0)
    def _():
        kp_ref[...] = jnp.zeros_like(kp_ref)
        vp_ref[...] = jnp.zeros_like(vp_ref)

    qi = lax.broadcasted_iota(jnp.int32, (BAND, BAND), 0)
    kj = lax.broadcasted_iota(jnp.int32, (BAND, BAND), 1)
    steps_c = (qi - kj).astype(F32)
    steps_p = steps_c + float(BAND)
    valid_c = kj <= qi
    valid_p = kj >= qi
    lane = lax.broadcasted_iota(jnp.int32, (BAND, DIL_HD), 1)
    heads = [slice(h * DIL_HD, (h + 1) * DIL_HD) for h in range(DIL_HEADS)]
    n_sub = q_ref.shape[0] // BAND
    subs = [slice(t * BAND, (t + 1) * BAND) for t in range(n_sub)]
    q = q_ref[...]
    kall, vall = k_ref[...].astype(BF16), v_ref[...].astype(BF16)
    kprev = [kp_ref[...]] + [kall[t] for t in subs[:-1]]
    vprev = [vp_ref[...]] + [vall[t] for t in subs[:-1]]
    sc = [[lax.dot_general(q[t, sl], kall[t, sl], _NT, preferred_element_type=F32) for sl in heads]
          for t in subs]
    sp = [[lax.dot_general(q[t, sl], kprev[i][:, sl], _NT, preferred_element_type=F32)
           for sl in heads] for i, t in enumerate(subs)]
    pcs, pps, ms = [], [], []
    for i in range(n_sub):
        ok_p = jnp.logical_and(valid_p, ut > 0) if i == 0 else valid_p
        for h in range(DIL_HEADS):
            slope = 2.0 ** -(h + 1)
            c = jnp.where(valid_c, sc[i][h] - slope * steps_c, NEG)
            p = jnp.where(ok_p, sp[i][h] - slope * steps_p, NEG)
            m = jnp.max(jnp.maximum(c, p), axis=-1, keepdims=True)
            pcs.append(jnp.exp(c - m).astype(BF16))
            pps.append(jnp.exp(p - m).astype(BF16))
            ms.append(m)
    ones = jnp.ones((BAND, DIL_HD), BF16)
    ext = [_dot(pcs[i * DIL_HEADS + h], jnp.concatenate([vall[t, sl], ones], axis=1))
           + _dot(pps[i * DIL_HEADS + h], jnp.concatenate([vprev[i][:, sl], ones], axis=1))
           for i, t in enumerate(subs) for h, sl in enumerate(heads)]
    for i, t in enumerate(subs):
        lse_all = jnp.zeros((BAND, DIL_HD), F32)
        for h, sl in enumerate(heads):
            e = ext[i * DIL_HEADS + h]
            l = e[:, DIL_HD:]
            o_ref[t, sl] = e[:, :DIL_HD] / l
            lse_all = jnp.where(lane == h, ms[i * DIL_HEADS + h] + jnp.log(l), lse_all)
        lse_ref[t, :] = lse_all
    kp_ref[...] = kall[subs[-1]]
    vp_ref[...] = vall[subs[-1]]


def _attn_prompt_group(q, kv, gi):
    batch, dil, ls, _ = q.shape
    rows = 2 * BAND
    assert ls % rows == 0, (ls, rows)
    tile = pl.BlockSpec((None, None, rows, DIL_DIM), lambda b, r, u: (b, r, u, 0))
    v_tile = pl.BlockSpec((None, None, rows, DIL_DIM), lambda b, r, u: (b, r, u, 1))
    return pl.pallas_call(
        _attn_prompt_kernel,
        grid=(batch, dil, ls // rows),
        in_specs=[tile, tile, v_tile],
        out_specs=[tile, pl.BlockSpec((None, None, rows, DIL_HD), lambda b, r, u: (b, r, u, 0))],
        out_shape=[
            jax.ShapeDtypeStruct((batch, dil, ls, DIL_DIM), F32),
            jax.ShapeDtypeStruct((batch, dil, ls, DIL_HD), F32),
        ],
        scratch_shapes=[pltpu.VMEM((BAND, DIL_DIM), BF16), pltpu.VMEM((BAND, DIL_DIM), BF16)],
        compiler_params=_params("parallel", "parallel", "arbitrary"),
        name="attn_prompt_g%d" % gi,
    )(q, kv, kv)


def _merge_matmul_res_kernel(o0, o1, o2, l0, l1, l2, w_ref, r_ref, out_ref, ot_ref, lt_ref):
    tm = out_ref.shape[0]
    sub = MERGE_SUB_ROWS
    wb = w_ref[...].astype(BF16)
    for s0 in range(0, tm, sub):
        rows = slice(s0, s0 + sub)
        for gi, (o_ref, l_ref) in enumerate(((o1, l1), (o2, l2))):
            dil = DIL_GROUPS[gi + 1][1]
            us = slice(s0 // dil, (s0 + sub) // dil)
            for r in range(dil):
                for h in range(DIL_HEADS):
                    ot_ref[gi, h, pl.ds(s0 + r, sub // dil, stride=dil), :] = o_ref[
                        r, us, h * DIL_HD:(h + 1) * DIL_HD]
                lt_ref[gi, pl.ds(s0 + r, sub // dil, stride=dil), :] = l_ref[r, us, :]
        a0, a1, a2 = l0[0, rows, :], lt_ref[0, rows, :], lt_ref[1, rows, :]
        m = jnp.maximum(jnp.maximum(a0, a1), a2)
        e0, e1, e2 = jnp.exp(a0 - m), jnp.exp(a1 - m), jnp.exp(a2 - m)
        inv = 1.0 / (e0 + e1 + e2)
        w0, w1, w2 = e0 * inv, e1 * inv, e2 * inv
        parts = []
        for h in range(DIL_HEADS):
            sl = slice(h * DIL_HD, (h + 1) * DIL_HD)
            parts.append((o0[0, rows, sl] * w0[:, h:h + 1] + ot_ref[0, h, rows, :] * w1[:, h:h + 1]
                          + ot_ref[1, h, rows, :] * w2[:, h:h + 1]).astype(BF16))
        out_ref[rows, :] = r_ref[rows, :] + _dot(jnp.concatenate(parts, axis=1), wb)


def _merge_matmul_res(os, lses, w_arr, layer, res, seq, tm):
    m = res.shape[0]
    nb = seq // tm
    in_specs = []
    for width in (DIL_DIM, DIL_HD):
        for _, dil in DIL_GROUPS:
            in_specs.append(pl.BlockSpec((None, dil, tm // dil, width),
                                         lambda i: (i // nb, 0, i % nb, 0)))
    in_specs += [
        pl.BlockSpec((None, DIL_DIM, D_MODEL), lambda i: (layer, 0, 0), pipeline_mode=pl.Buffered(1)),
        pl.BlockSpec((tm, D_MODEL), lambda i: (i, 0)),
    ]
    return pl.pallas_call(
        _merge_matmul_res_kernel,
        grid=(m // tm,),
        in_specs=in_specs,
        out_specs=pl.BlockSpec((tm, D_MODEL), lambda i: (i, 0)),
        out_shape=jax.ShapeDtypeStruct((m, D_MODEL), F32),
        scratch_shapes=[
            pltpu.VMEM((N_GROUPS - 1, DIL_HEADS, tm, DIL_HD), F32),
            pltpu.VMEM((N_GROUPS - 1, tm, DIL_HD), F32),
        ],
        compiler_params=_params("parallel"),
        name="merge_matmul_res",
    )(*os, *lses, w_arr, res)


def _attn_sample_kernel(q_ref, kvn_ref, c0_ref, c1_ref, c2_ref, o_ref):
    caches = (c0_ref, c1_ref, c2_ref)
    steps = float(BAND) - lax.broadcasted_iota(jnp.int32, (BAND, 1, 1), 0).astype(F32)
    head = lax.broadcasted_iota(jnp.int32, (1, DIL_HEADS, 1), 1)
    slope = jnp.zeros((1, DIL_HEADS, 1), F32)
    for h in range(DIL_HEADS):
        slope = jnp.where(head == h, 2.0 ** -(h + 1), slope)
    bias = slope * steps
    outs, lses = [], []
    for g in range(N_GROUPS):
        c_ref = caches[g]
        q = q_ref[g]
        kn, vn = kvn_ref[g, 0], kvn_ref[g, 1]
        sb = jnp.sum(c_ref[:, 0] * q[None], axis=-1, keepdims=True) - bias
        sn = jnp.sum(kn * q, axis=-1, keepdims=True)
        m = jnp.maximum(jnp.max(sb, axis=0), sn)
        pb = jnp.exp(sb - m[None])
        pn = jnp.exp(sn - m)
        l = jnp.sum(pb, axis=0) + pn
        outs.append((jnp.sum(pb * c_ref[:, 1], axis=0) + pn * vn) / l)
        lses.append(m + jnp.log(l))
    m = jnp.maximum(jnp.maximum(lses[0], lses[1]), lses[2])
    es = [jnp.exp(a - m) for a in lses]
    inv = 1.0 / (es[0] + es[1] + es[2])
    o_ref[...] = outs[0] * (es[0] * inv) + outs[1] * (es[1] * inv) + outs[2] * (es[2] * inv)


def _attn_sample(q, kv_new, caches):
    b = q.shape[0]
    views, cache_specs = [], []
    for cache, (_, dil) in zip(caches, DIL_GROUPS):
        wb = cache.shape[1]
        views.append(cache.reshape(b, wb // dil, dil, 2, DIL_HEADS, DIL_HD))
        cache_specs.append(pl.BlockSpec((None, BAND, None, 2, DIL_HEADS, DIL_HD),
                                        lambda i: (i, 0, 0, 0, 0, 0)))
    o = pl.pallas_call(
        _attn_sample_kernel,
        grid=(b,),
        in_specs=[
            pl.BlockSpec((None, N_GROUPS, DIL_HEADS, DIL_HD), lambda i: (i, 0, 0, 0)),
            pl.BlockSpec((None, N_GROUPS, 2, DIL_HEADS, DIL_HD), lambda i: (i, 0, 0, 0, 0)),
        ] + cache_specs,
        out_specs=pl.BlockSpec((None, DIL_HEADS, DIL_HD), lambda i: (i, 0, 0)),
        out_shape=jax.ShapeDtypeStruct((b, DIL_HEADS, DIL_HD), F32),
        compiler_params=_params("parallel"),
        name="attn_sample",
    )(q.reshape(b, N_GROUPS, DIL_HEADS, DIL_HD),
      kv_new.reshape(b, N_GROUPS, 2, DIL_HEADS, DIL_HD), *views)
    return o.reshape(b, DIL_DIM)


def _trunk(x, xs, batch, seq, hg_state, caches, wts):
    (norm_g, ffn_w_in, ffn_w_out, hg_w_in, hg_w_out, lbs, hg_out_g, kv_norm_g, w_kv,
     k_norm_g, dil_w_q, dil_q_norm_g, dil_w_o) = wts
    ns = xs.shape[0]
    tm, tn = 512, 1024
    states_p, states_s = [], []
    kv_p = kv_s = None
    gain = lambda layer, idx: pl.BlockSpec((None, None, 1, D_MODEL), lambda i, j: (layer, idx, 0, 0))
    to_cast = [(hg_w_in, (layer,)) for layer in range(N_A_LAYERS)] + [(w_kv, ())]
    casted = []

    def ffn(x, xs, layer, half, gain_idx):
        cast = to_cast[len(casted)] if len(casted) < len(to_cast) else None
        outs = _ffn(x, xs, norm_g, ffn_w_in, ffn_w_out, layer, half, gain_idx, cast)
        if cast is not None:
            casted.append(outs[2])
        return outs[0], outs[1]

    for layer in range(DEPTH):
        x, xs = ffn(x, xs, layer, 0, 0)
        if layer < N_A_LAYERS:
            hg_w = casted[layer]
            w_spec = pl.BlockSpec((D_MODEL, tn), lambda i, j: (0, j))
            lb = lbs[layer].reshape(HG_HEADS, HG_DK)
            g_out = hg_out_g[layer].reshape(1, HG_DK)
            wide_spec = pl.BlockSpec((D_MODEL, 2 * tn), lambda i, j: (0, j))
            proj = _norm_matmul(x, norm_g, gain(layer, 1), hg_w, wide_spec,
                                4 * D_MODEL, k_norm_g, "plain", 2 * tm, 2 * tn)
            o, s = _hgrn_scan(proj, lb, g_out, batch, seq)
            states_p.append(s)
            x = _matmul_res(o, hg_w_out, layer, x, tm)
            proj = _norm_matmul(xs, norm_g, gain(layer, 1), hg_w, w_spec,
                                4 * D_MODEL, k_norm_g, "plain", ns, tn)
            o, s = _hgrn_step(proj, hg_state, layer, lb, g_out)
            states_s.append(s)
            xs = _matmul_res(o, hg_w_out, layer, xs, ns)
        else:
            bi = layer - N_A_LAYERS
            qs = _q_streams(x, norm_g,
                            pl.BlockSpec((None, None, 1, D_MODEL), lambda i, layer=layer: (layer, 1, 0, 0)),
                            dil_w_q, bi, dil_q_norm_g[bi], batch, seq, tm)
            parts = [_attn_prompt_group(qs[gi], kv_p[gi], gi) for gi in range(N_GROUPS)]
            x = _merge_matmul_res([p[0] for p in parts], [p[1] for p in parts],
                                  dil_w_o, bi, x, seq, tm)
            w_spec = pl.BlockSpec((None, D_MODEL, DIL_DIM), lambda i, j, bi=bi: (bi, 0, j))
            q = _norm_matmul(xs, norm_g, gain(layer, 1), dil_w_q, w_spec,
                             N_GROUPS * DIL_DIM, dil_q_norm_g[bi], "q", ns, DIL_DIM)
            xs = _matmul_res(_attn_sample(q, kv_s, caches), dil_w_o, bi, xs, ns)
        x, xs = ffn(x, xs, layer, 1, 2)
        if layer == N_A_LAYERS - 1:
            kv_gain = kv_norm_g.reshape(1, D_MODEL)
            kv_gain_spec = pl.BlockSpec((1, D_MODEL), lambda i, j: (0, 0))
            w_spec = pl.BlockSpec((D_MODEL, DIL_DIM), lambda i, j: (0, j))
            kv_w = casted[N_A_LAYERS]
            kv_p = _kv_streams(x, kv_gain, kv_w, k_norm_g, batch, seq, tm // 2)
            kv_s = _norm_matmul(xs, kv_gain, kv_gain_spec, kv_w, w_spec,
                                2 * N_GROUPS * DIL_DIM, k_norm_g, "kv", ns, DIL_DIM)
    return x, xs, jnp.stack(states_p, axis=0), jnp.stack(states_s, axis=0), kv_p, kv_s


def kernel(x_prompt, x_sample, state_hgrn, cache_kv_w128, cache_kv_w512, cache_kv_w2048,
           norm_g, ffn_w_in, ffn_w_out, hg_w_in, hg_w_out, hg_lb_logits, hg_out_g,
           kv_norm_g, w_kv, k_norm_g, dil_w_q, dil_q_norm_g, dil_w_o):
    batch, seq, _ = x_prompt.shape
    dec_batch = x_sample.shape[0]
    p = jax.nn.softmax(hg_lb_logits.astype(F32), axis=0)
    lbs = jnp.cumsum(p, axis=0) - p[0]
    wts = (norm_g.reshape(DEPTH, 3, 1, D_MODEL), ffn_w_in, ffn_w_out,
           hg_w_in, hg_w_out, lbs, hg_out_g, kv_norm_g,
           w_kv, k_norm_g, dil_w_q, dil_q_norm_g, dil_w_o)

    y_p, y_s, hg_p, hg_s, kv_p, kv_s = _trunk(
        x_prompt.reshape(batch * seq, D_MODEL), x_sample.reshape(dec_batch, D_MODEL), batch, seq,
        state_hgrn, (cache_kv_w128, cache_kv_w512, cache_kv_w2048), wts)

    kv_s = kv_s.reshape(dec_batch, 1, N_GROUPS, 2, DIL_HEADS, DIL_HD)
    outs = [y_p.reshape(batch, seq, D_MODEL), y_s.reshape(dec_batch, 1, D_MODEL), hg_p, hg_s]
    for gi, (win, dil) in enumerate(DIL_GROUPS):
        rows = min(win, seq)
        tail = jnp.swapaxes(kv_p[gi][:, :, (seq - rows) // dil:], 1, 2)
        outs.append(tail.reshape(batch, rows, 2, DIL_HEADS, DIL_HD))
        outs.append(kv_s[:, :, gi])
    return tuple(outs)
```

```python
import functools

import jax
import jax.numpy as jnp
from jax import lax
from jax.experimental import pallas as pl
from jax.experimental.pallas import tpu as pltpu

D_MODEL = 2048
D_FF = 5632
DEPTH = 4
N_A_LAYERS = 2
HG_HEADS = 16
HG_DK = 128
DIL_GROUPS = ((128, 1), (512, 4), (2048, 16))
N_GROUPS = 3
DIL_HEADS = 8
DIL_HD = 128
DIL_DIM = DIL_HEADS * DIL_HD
BAND = 128
EPS = 1e-6
NEG = -1e30
F32 = jnp.float32
BF16 = jnp.bfloat16

VMEM_LIMIT_BYTES = 56 * 1024 * 1024
BF16_SUBLANES = 16
LANES = 128
HG_CHUNK = 32
HG_TILE = 512
HG_HEADS_PER_PROGRAM = 2
HG_CUM_BLOCK = 256
MERGE_SUB_ROWS = 256
ATTN_MAX_SUBTILES = 4
HG_SAFE_DECAY = 80.0

_NT = (((1,), (1,)), ((), ()))
_TN = (((0,), (0,)), ((), ()))


def _params(*semantics):
    return pltpu.CompilerParams(dimension_semantics=semantics, vmem_limit_bytes=VMEM_LIMIT_BYTES)


def _rms_scale(x):
    return lax.rsqrt(jnp.mean(x * x, axis=-1, keepdims=True) + EPS)


def _silu(x):
    return x / (1.0 + jnp.exp(-x))


def _dot(a, b):
    return jnp.dot(a, b, preferred_element_type=F32)


def _ffn_kernel(x_ref, xs_ref, g_ref, wg_ref, wu_ref, wo_ref, *rest, n_cast):
    if n_cast:
        cast_in_ref, o_ref, os_ref, cast_out_ref, xn_ref, accs_ref = rest
    else:
        o_ref, os_ref, xn_ref, accs_ref = rest
    j = pl.program_id(1)
    tm = x_ref.shape[0]
    ns = xs_ref.shape[0]

    if n_cast:
        @pl.when(pl.program_id(0) * pl.num_programs(1) + j < n_cast)
        def _():
            cast_out_ref[...] = cast_in_ref[...].astype(BF16)

    @pl.when(j == 0)
    def _():
        g = g_ref[...]
        x = x_ref[...]
        xn_ref[0:tm, :] = (x * _rms_scale(x) * g).astype(BF16)
        xs = xs_ref[...]
        xsn = jnp.concatenate(
            [xs * _rms_scale(xs) * g, jnp.zeros((BF16_SUBLANES - ns, D_MODEL), F32)], axis=0)
        xn_ref[tm:tm + BF16_SUBLANES, :] = xsn.astype(BF16)
        o_ref[...] = jnp.zeros_like(o_ref)
        accs_ref[...] = jnp.zeros_like(accs_ref)

    def step(rows):
        xn = xn_ref[0:rows, :]
        gate = _dot(xn, wg_ref[...].astype(BF16))
        up = _dot(xn, wu_ref[...].astype(BF16))
        act = (_silu(gate) * up).astype(BF16)
        return _dot(act, wo_ref[...].astype(BF16))

    first_tile = pl.program_id(0) == 0

    @pl.when(first_tile)
    def _():
        res = step(tm + BF16_SUBLANES)
        o_ref[...] += res[0:tm]
        accs_ref[...] += res[tm:tm + BF16_SUBLANES]

    @pl.when(jnp.logical_not(first_tile))
    def _():
        o_ref[...] += step(tm)

    last = j == pl.num_programs(1) - 1

    @pl.when(last)
    def _():
        o_ref[...] = x_ref[...] + 0.5 * o_ref[...]

    @pl.when(jnp.logical_and(last, first_tile))
    def _():
        os_ref[...] = xs_ref[...] + 0.5 * accs_ref[0:ns, :]


def _ffn(x, xs, norm_g, w_in, w_out, layer, half, gain_idx, cast=None, tm=1024, tf=256):
    m = x.shape[0]
    ns = xs.shape[0]
    nj = D_FF // tf
    in_specs = [
        pl.BlockSpec((tm, D_MODEL), lambda i, j: (i, 0)),
        pl.BlockSpec((ns, D_MODEL), lambda i, j: (0, 0)),
        pl.BlockSpec((None, None, 1, D_MODEL), lambda i, j: (layer, gain_idx, 0, 0)),
        pl.BlockSpec((None, None, D_MODEL, tf), lambda i, j: (layer, half, 0, j)),
        pl.BlockSpec((None, None, D_MODEL, tf), lambda i, j: (layer, half, 0, j + nj)),
        pl.BlockSpec((None, None, tf, D_MODEL), lambda i, j: (layer, half, j, 0)),
    ]
    out_specs = [
        pl.BlockSpec((tm, D_MODEL), lambda i, j: (i, 0)),
        pl.BlockSpec((ns, D_MODEL), lambda i, j: (0, 0)),
    ]
    out_shape = [
        jax.ShapeDtypeStruct((m, D_MODEL), F32),
        jax.ShapeDtypeStruct((ns, D_MODEL), F32),
    ]
    args = [x, xs, norm_g, w_in, w_in, w_out]
    n_cast = 0
    if cast is not None:
        cast_w, lead = cast
        n_cols = cast_w.shape[-1]
        n_cast = n_cols // LANES
        assert n_cast <= (m // tm) * nj, (n_cast, m // tm, nj)
        blk = lambda i, j: jnp.minimum(i * nj + j, n_cast - 1)
        in_specs.append(pl.BlockSpec((None,) * len(lead) + (D_MODEL, LANES),
                                     lambda i, j: lead + (0, blk(i, j))))
        out_specs.append(pl.BlockSpec((D_MODEL, LANES), lambda i, j: (0, blk(i, j))))
        out_shape.append(jax.ShapeDtypeStruct((D_MODEL, n_cols), BF16))
        args.append(cast_w)
    return pl.pallas_call(
        functools.partial(_ffn_kernel, n_cast=n_cast),
        grid=(m // tm, nj),
        in_specs=in_specs,
        out_specs=out_specs,
        out_shape=out_shape,
        scratch_shapes=[
            pltpu.VMEM((tm + BF16_SUBLANES, D_MODEL), BF16),
            pltpu.VMEM((BF16_SUBLANES, D_MODEL), F32),
        ],
        compiler_params=_params("arbitrary", "arbitrary"),
        name="ffn",
    )(*args)


def _head_norm(y, gain, scale):
    parts = []
    for h in range(DIL_HEADS):
        yh = y[:, h * DIL_HD:(h + 1) * DIL_HD]
        parts.append(yh * _rms_scale(yh) * gain * scale)
    return jnp.concatenate(parts, axis=-1)


def _norm_matmul_kernel(x_ref, g_ref, w_ref, hg_ref, o_ref, xn_ref, *, mode):
    j = pl.program_id(1)

    @pl.when(j == 0)
    def _():
        x = x_ref[...]
        xn_ref[...] = (x * _rms_scale(x) * g_ref[...]).astype(BF16)

    y = _dot(xn_ref[...], w_ref[...].astype(BF16))
    if mode == "plain":
        o_ref[...] = y
    elif mode == "q":
        o_ref[...] = _head_norm(y, hg_ref[pl.ds(j, 1), :], DIL_HD ** -0.5)
    else:
        @pl.when(j % 2 == 0)
        def _():
            o_ref[...] = _head_norm(y, hg_ref[pl.ds(j // 2, 1), :], 1.0)

        @pl.when(j % 2 == 1)
        def _():
            o_ref[...] = y


def _norm_matmul(x, gain_arr, gain_spec, w_arr, w_spec, n, head_gain, mode, tm, tn):
    m = x.shape[0]
    return pl.pallas_call(
        functools.partial(_norm_matmul_kernel, mode=mode),
        grid=(m // tm, n // tn),
        in_specs=[
            pl.BlockSpec((tm, D_MODEL), lambda i, j: (i, 0)),
            gain_spec,
            w_spec,
            pl.BlockSpec(head_gain.shape, lambda i, j: (0, 0)),
        ],
        out_specs=pl.BlockSpec((tm, tn), lambda i, j: (i, j)),
        out_shape=jax.ShapeDtypeStruct((m, n), F32),
        scratch_shapes=[pltpu.VMEM((tm, D_MODEL), BF16)],
        compiler_params=_params("parallel", "arbitrary"),
        name="norm_matmul_" + mode,
    )(x, gain_arr, w_arr, head_gain)


def _store_stream_major(y, gain, scale, out, col0, y_ref, dil):
    tm = y.shape[0]
    for h in range(DIL_HEADS):
        piece = y[:, h * DIL_HD:(h + 1) * DIL_HD]
        if gain is not None:
            piece = piece * _rms_scale(piece) * gain * scale
        if dil == 1:
            out[0, :, col0 + h * DIL_HD:col0 + (h + 1) * DIL_HD] = piece.astype(out.dtype)
        else:
            y_ref[h] = piece
    if dil > 1:
        for r in range(dil):
            for h in range(DIL_HEADS):
                out[r, :, col0 + h * DIL_HD:col0 + (h + 1) * DIL_HD] = y_ref[
                    h, pl.ds(r, tm // dil, stride=dil), :].astype(out.dtype)


def _kv_streams_kernel(x_ref, g_ref, w_ref, hg_ref, o0_ref, o1_ref, o2_ref, y_ref):
    outs = (o0_ref, o1_ref, o2_ref)
    x = x_ref[...]
    xn = (x * _rms_scale(x) * g_ref[...]).astype(BF16)
    for jj in [2 * g + kv for g in reversed(range(N_GROUPS)) for kv in range(2)]:
        group, kv = jj // 2, jj % 2
        y = _dot(xn, w_ref[:, jj * DIL_DIM:(jj + 1) * DIL_DIM])
        gain = hg_ref[group:group + 1, :] if kv == 0 else None
        _store_stream_major(y, gain, 1.0, outs[group], kv * DIL_DIM, y_ref.at[jj],
                            DIL_GROUPS[group][1])


def _kv_streams(x, gain_arr, w_arr, head_gain, batch, seq, tm):
    nb = seq // tm
    out_specs, out_shapes = [], []
    for _, dil in DIL_GROUPS:
        out_specs.append(pl.BlockSpec((None, dil, tm // dil, 2 * DIL_DIM),
                                      lambda i: (i // nb, 0, i % nb, 0)))
        out_shapes.append(jax.ShapeDtypeStruct((batch, dil, seq // dil, 2 * DIL_DIM), F32))
    return pl.pallas_call(
        _kv_streams_kernel,
        grid=(batch * nb,),
        in_specs=[
            pl.BlockSpec((tm, D_MODEL), lambda i: (i, 0)),
            pl.BlockSpec((1, D_MODEL), lambda i: (0, 0)),
            pl.BlockSpec((D_MODEL, 2 * N_GROUPS * DIL_DIM), lambda i: (0, 0),
                         pipeline_mode=pl.Buffered(1)),
            pl.BlockSpec(head_gain.shape, lambda i: (0, 0)),
        ],
        out_specs=out_specs,
        out_shape=out_shapes,
        scratch_shapes=[pltpu.VMEM((2 * N_GROUPS, DIL_HEADS, tm, DIL_HD), F32)],
        compiler_params=_params("parallel"),
        name="kv_streams",
    )(x, gain_arr, w_arr, head_gain)


def _q_streams_kernel(x_ref, g_ref, w_ref, hg_ref, o0_ref, o1_ref, o2_ref, y_ref):
    x = x_ref[...]
    xn = (x * _rms_scale(x) * g_ref[...]).astype(BF16)
    for group, out in reversed(list(enumerate((o0_ref, o1_ref, o2_ref)))):
        y = _dot(xn, w_ref[:, group * DIL_DIM:(group + 1) * DIL_DIM].astype(BF16))
        _store_stream_major(y, hg_ref[group:group + 1, :], DIL_HD ** -0.5, out, 0,
                            y_ref.at[group], DIL_GROUPS[group][1])


def _q_streams(x, gain_arr, gain_spec, w_arr, bi, head_gain, batch, seq, tm):
    nb = seq // tm
    out_specs, out_shapes = [], []
    for _, dil in DIL_GROUPS:
        out_specs.append(pl.BlockSpec((None, dil, tm // dil, DIL_DIM),
                                      lambda i: (i // nb, 0, i % nb, 0)))
        out_shapes.append(jax.ShapeDtypeStruct((batch, dil, seq // dil, DIL_DIM), BF16))
    return pl.pallas_call(
        _q_streams_kernel,
        grid=(batch * nb,),
        in_specs=[
            pl.BlockSpec((tm, D_MODEL), lambda i: (i, 0)),
            gain_spec,
            pl.BlockSpec((None, D_MODEL, N_GROUPS * DIL_DIM), lambda i: (bi, 0, 0),
                         pipeline_mode=pl.Buffered(1)),
            pl.BlockSpec(head_gain.shape, lambda i: (0, 0)),
        ],
        out_specs=out_specs,
        out_shape=out_shapes,
        scratch_shapes=[pltpu.VMEM((N_GROUPS, DIL_HEADS, tm, DIL_HD), F32)],
        compiler_params=_params("parallel"),
        name="q_streams",
    )(x, gain_arr, w_arr, head_gain)


def _matmul_res_kernel(a_ref, w_ref, r_ref, o_ref):
    o_ref[...] = r_ref[...] + _dot(a_ref[...].astype(BF16), w_ref[...].astype(BF16))


def _matmul_res(a, w_arr, layer, res, tm):
    m, k = a.shape
    n = res.shape[1]
    return pl.pallas_call(
        _matmul_res_kernel,
        grid=(m // tm,),
        in_specs=[
            pl.BlockSpec((tm, k), lambda i: (i, 0)),
            pl.BlockSpec((None, k, n), lambda i: (layer, 0, 0), pipeline_mode=pl.Buffered(1)),
            pl.BlockSpec((tm, n), lambda i: (i, 0)),
        ],
        out_specs=pl.BlockSpec((tm, n), lambda i: (i, 0)),
        out_shape=jax.ShapeDtypeStruct((m, n), F32),
        compiler_params=_params("parallel"),
        name="matmul_res",
    )(a, w_arr, res)


def _hgrn_gates(z, lb):
    e = jnp.exp(-jnp.abs(z))
    inv = 1.0 / (1.0 + e)
    pos = z >= 0
    sig = jnp.where(pos, inv, e * inv)
    sig_neg = jnp.where(pos, e * inv, inv)
    k = (1.0 - lb) * sig_neg
    f = lb + (1.0 - lb) * sig
    return f, k, jnp.log(jnp.maximum(f, 1e-30))


def _hgrn_pairwise_chunk(q, k, v, cum):
    c = q.shape[0]
    n_sub = c // 8
    sub_row = lax.broadcasted_iota(jnp.int32, (8, 1), 0)
    acc = [jnp.zeros((8, HG_DK), F32) for _ in range(n_sub)]
    for s in range(c):
        cs = cum[s:s + 1, :]
        ks = k[s:s + 1, :]
        vs = v[s:s + 1, :]
        for tg in range(s // 8, n_sub):
            rows = slice(tg * 8, (tg + 1) * 8)
            p = q[rows, :] * ks * jnp.exp(cum[rows, :] - cs)
            a = jnp.sum(p, axis=-1, keepdims=True)
            if tg == s // 8:
                a = jnp.where(sub_row >= (s - tg * 8), a, 0.0)
            acc[tg] = acc[tg] + a * vs
    return jnp.concatenate(acc, axis=0)


def _hgrn_scan_kernel(q_ref, z_ref, v_ref, gt_ref, lb_ref, go_ref, o_ref, s_ref,
                      st_ref, oi_ref, k_ref, cum_ref):
    c = HG_CHUNK
    tile = HG_TILE
    n_chunks = tile // c
    n_par = HG_HEADS_PER_PROGRAM
    hp = pl.program_id(1)
    lbs = [lb_ref[pl.ds(hp * n_par + p, 1), :] for p in range(n_par)]
    lanes = [slice(p * HG_DK, (p + 1) * HG_DK) for p in range(n_par)]
    heads = range(n_par)
    g_out = go_ref[...]
    blk = HG_CUM_BLOCK
    row = lax.broadcasted_iota(jnp.int32, (blk, blk), 0)
    col = lax.broadcasted_iota(jnp.int32, (blk, blk), 1)
    shift = c.bit_length() - 1
    same_chunk = jnp.right_shift(row, shift) == jnp.right_shift(col, shift)
    tri = jnp.where(jnp.logical_and(same_chunk, col <= row), 1.0, 0.0).astype(BF16)
    crow = lax.broadcasted_iota(jnp.int32, (c, c), 0)
    ccol = lax.broadcasted_iota(jnp.int32, (c, c), 1)
    causal = ccol <= crow
    st_ref[...] = jnp.zeros_like(st_ref)

    def body(ti, carry):
        r0 = pl.multiple_of(ti * tile, tile)
        rows = pl.ds(r0, tile)
        chunks = [slice(ci * c, (ci + 1) * c) for ci in range(n_chunks)]
        q = [q_ref[rows, ln] for ln in lanes]
        vb = [v_ref[rows, ln].astype(BF16) for ln in lanes]
        k, split = [], []
        for p in heads:
            _, kp, lf = _hgrn_gates(z_ref[rows, lanes[p]], lbs[p])
            k.append(kp)
            hi = lf.astype(BF16)
            rem = lf - hi.astype(F32)
            mid = rem.astype(BF16)
            split += [hi, mid, (rem - mid.astype(F32)).astype(BF16)]
        split = jnp.concatenate(split, axis=1)
        cum_all = jnp.concatenate(
            [_dot(tri, split[b0:b0 + blk]) for b0 in range(0, tile, blk)], axis=0)
        cum = [cum_all[:, (3 * p) * HG_DK:(3 * p + 1) * HG_DK]
               + cum_all[:, (3 * p + 1) * HG_DK:(3 * p + 2) * HG_DK]
               + cum_all[:, (3 * p + 2) * HG_DK:(3 * p + 3) * HG_DK] for p in heads]
        lasts = [[cum[p][sl.stop - 1:sl.stop, :] for sl in chunks] for p in heads]
        qdb = [(q[p] * jnp.exp(cum[p])).astype(BF16) for p in heads]
        khb = [(k[p] * jnp.exp(-cum[p])).astype(BF16) for p in heads]
        atts = [[lax.dot_general(qdb[p][sl], khb[p][sl], _NT, preferred_element_type=F32)
                 for sl in chunks] for p in heads]
        atts = [[jnp.where(causal, att, 0.0).astype(BF16) for att in atts[p]] for p in heads]
        intra = [[_dot(att, vb[p][sl]) for att, sl in zip(atts[p], chunks)] for p in heads]
        kdb = [jnp.concatenate([k[p][sl] * jnp.exp(last - cum[p][sl])
                                for sl, last in zip(chunks, lasts[p])], axis=0).astype(BF16)
               for p in heads]
        incs = [[lax.dot_general(vb[p][sl], kdb[p][sl], _TN, preferred_element_type=F32)
                 for sl in chunks] for p in heads]
        sts = [[st_ref[p]] for p in heads]
        for p in heads:
            for inc, last in zip(incs[p], lasts[p]):
                sts[p].append(sts[p][-1] * jnp.exp(last) + inc)
            st_ref[p] = sts[p][-1]
        inter = [jnp.concatenate(
            [lax.dot_general(qdb[p][sl], st.astype(BF16), _NT, preferred_element_type=F32)
             for sl, st in zip(chunks, sts[p])], axis=0) for p in heads]
        for p in heads:
            oi_ref[p] = jnp.concatenate(intra[p], axis=0) + inter[p]

        cum_min = cum[0]
        for p in heads[1:]:
            cum_min = jnp.minimum(cum_min, cum[p])

        @pl.when(jnp.min(cum_min) <= -HG_SAFE_DECAY)
        def _():
            for p in heads:
                k_ref[...] = k[p]
                cum_ref[...] = cum[p]
                oi_ref[p] = inter[p]

                def chunk(ci, carry2, p=p):
                    c0 = pl.multiple_of(ci * c, c)
                    oi_ref[p, pl.ds(c0, c), :] += _hgrn_pairwise_chunk(
                        q_ref[pl.ds(r0 + c0, c), lanes[p]], k_ref[pl.ds(c0, c), :],
                        v_ref[pl.ds(r0 + c0, c), lanes[p]], cum_ref[pl.ds(c0, c), :])
                    return carry2

                lax.fori_loop(0, n_chunks, chunk, 0)

        for p in heads:
            o = oi_ref[p]
            o = o * _rms_scale(o) * g_out * _silu(gt_ref[rows, lanes[p]])
            o_ref[rows, lanes[p]] = o.astype(o_ref.dtype)
        return carry

    lax.fori_loop(0, q_ref.shape[0] // tile, body, 0)
    for p in heads:
        s_ref[p] = st_ref[p].T


def _hgrn_scan(proj, lb, g_out, batch, seq):
    n_par = HG_HEADS_PER_PROGRAM
    n_prog = HG_HEADS // n_par
    width = n_par * HG_DK
    col = lambda k: (lambda b, h: (b, k * n_prog + h))
    return pl.pallas_call(
        _hgrn_scan_kernel,
        grid=(batch, n_prog),
        in_specs=[
            pl.BlockSpec((seq, width), col(0)),
            pl.BlockSpec((seq, width), col(1)),
            pl.BlockSpec((seq, width), col(2)),
            pl.BlockSpec((seq, width), col(3)),
            pl.BlockSpec((HG_HEADS, HG_DK), lambda b, h: (0, 0)),
            pl.BlockSpec((1, HG_DK), lambda b, h: (0, 0)),
        ],
        out_specs=[
            pl.BlockSpec((seq, width), lambda b, h: (b, h)),
            pl.BlockSpec((None, n_par, HG_DK, HG_DK), lambda b, h: (b, h, 0, 0)),
        ],
        out_shape=[
            jax.ShapeDtypeStruct((batch * seq, D_MODEL), BF16),
            jax.ShapeDtypeStruct((batch, HG_HEADS, HG_DK, HG_DK), F32),
        ],
        scratch_shapes=[
            pltpu.VMEM((n_par, HG_DK, HG_DK), F32),
            pltpu.VMEM((n_par, HG_TILE, HG_DK), F32),
            pltpu.VMEM((HG_TILE, HG_DK), F32),
            pltpu.VMEM((HG_TILE, HG_DK), F32),
        ],
        compiler_params=_params("parallel", "parallel"),
        name="hgrn_scan",
    )(proj, proj, proj, proj, lb, g_out)


def _hgrn_step_kernel(x_ref, s0_ref, lb_ref, go_ref, o_ref, s_ref):
    nh = HG_HEADS
    x = x_ref[...]
    q = x[0:nh]
    f, k, _ = _hgrn_gates(x[nh:2 * nh], lb_ref[...])
    r = lax.broadcasted_iota(jnp.int32, (HG_DK, HG_DK), 0)
    c = lax.broadcasted_iota(jnp.int32, (HG_DK, HG_DK), 1)
    eye = (r == c).astype(F32)
    cols = lax.dot_general(eye, jnp.concatenate([q, f, k], axis=0), _NT,
                           preferred_element_type=F32, precision=lax.Precision.HIGHEST)
    g_out = go_ref[...]
    for h in range(nh):
        qc = cols[:, h:h + 1]
        fc = cols[:, nh + h:nh + h + 1]
        kc = cols[:, 2 * nh + h:2 * nh + h + 1]
        v = x[2 * nh + h:2 * nh + h + 1, :]
        s_new = fc * s0_ref[h] + kc * v
        s_ref[h] = s_new
        o = jnp.sum(s_new * qc, axis=0, keepdims=True)
        o = o * _rms_scale(o) * g_out * _silu(x[3 * nh + h:3 * nh + h + 1, :])
        o_ref[:, h * HG_DK:(h + 1) * HG_DK] = o


def _hgrn_step(proj, state, layer, lb, g_out):
    b = proj.shape[0]
    nh = HG_HEADS
    o, s = pl.pallas_call(
        _hgrn_step_kernel,
        grid=(b,),
        in_specs=[
            pl.BlockSpec((None, 4 * nh, HG_DK), lambda i: (i, 0, 0)),
            pl.BlockSpec((None, None, nh, HG_DK, HG_DK), lambda i: (layer, i, 0, 0, 0)),
            pl.BlockSpec((nh, HG_DK), lambda i: (0, 0)),
            pl.BlockSpec((1, HG_DK), lambda i: (0, 0)),
        ],
        out_specs=[
            pl.BlockSpec((None, 1, D_MODEL), lambda i: (i, 0, 0)),
            pl.BlockSpec((None, nh, HG_DK, HG_DK), lambda i: (i, 0, 0, 0)),
        ],
        out_shape=[
            jax.ShapeDtypeStruct((b, 1, D_MODEL), F32),
            jax.ShapeDtypeStruct((b, nh, HG_DK, HG_DK), F32),
        ],
        compiler_params=_params("parallel"),
        name="hgrn_step",
    )(proj.reshape(b, 4 * nh, HG_DK), state, lb, g_out)
    return o.reshape(b, D_MODEL), s


def _attn_prompt_kernel(q_ref, k_ref, v_ref, o_ref, lse_ref, kp_ref, vp_ref):
    ut = pl.program_id(2)

    @pl.when(ut == 0)
    def _():
        kp_ref[...] = jnp.zeros_like(kp_ref)
        vp_ref[...] = jnp.zeros_like(vp_ref)

    qi = lax.broadcasted_iota(jnp.int32, (BAND, BAND), 0)
    kj = lax.broadcasted_iota(jnp.int32, (BAND, BAND), 1)
    steps_c = (qi - kj).astype(F32)
    steps_p = steps_c + float(BAND)
    valid_c = kj <= qi
    valid_p = kj >= qi
    lane = lax.broadcasted_iota(jnp.int32, (BAND, DIL_HD), 1)
    heads = [slice(h * DIL_HD, (h + 1) * DIL_HD) for h in range(DIL_HEADS)]
    n_sub = q_ref.shape[0] // BAND
    subs = [slice(t * BAND, (t + 1) * BAND) for t in range(n_sub)]
    q = q_ref[...]
    kall, vall = k_ref[...].astype(BF16), v_ref[...].astype(BF16)
    kprev = [kp_ref[...]] + [kall[t] for t in subs[:-1]]
    vprev = [vp_ref[...]] + [vall[t] for t in subs[:-1]]
    sc = [[lax.dot_general(q[t, sl], kall[t, sl], _NT, preferred_element_type=F32) for sl in heads]
          for t in subs]
    sp = [[lax.dot_general(q[t, sl], kprev[i][:, sl], _NT, preferred_element_type=F32)
           for sl in heads] for i, t in enumerate(subs)]
    pcs, pps, ms = [], [], []
    for i in range(n_sub):
        ok_p = jnp.logical_and(valid_p, ut > 0) if i == 0 else valid_p
        for h in range(DIL_HEADS):
            slope = 2.0 ** -(h + 1)
            c = jnp.where(valid_c, sc[i][h] - slope * steps_c, NEG)
            p = jnp.where(ok_p, sp[i][h] - slope * steps_p, NEG)
            m = jnp.max(jnp.maximum(c, p), axis=-1, keepdims=True)
            pcs.append(jnp.exp(c - m).astype(BF16))
            pps.append(jnp.exp(p - m).astype(BF16))
            ms.append(m)
    ones = jnp.ones((BAND, DIL_HD), BF16)
    ext = [_dot(pcs[i * DIL_HEADS + h], jnp.concatenate([vall[t, sl], ones], axis=1))
           + _dot(pps[i * DIL_HEADS + h], jnp.concatenate([vprev[i][:, sl], ones], axis=1))
           for i, t in enumerate(subs) for h, sl in enumerate(heads)]
    for i, t in enumerate(subs):
        lse_all = jnp.zeros((BAND, DIL_HD), F32)
        for h, sl in enumerate(heads):
            e = ext[i * DIL_HEADS + h]
            l = e[:, DIL_HD:]
            o_ref[t, sl] = e[:, :DIL_HD] / l
            lse_all = jnp.where(lane == h, ms[i * DIL_HEADS + h] + jnp.log(l), lse_all)
        lse_ref[t, :] = lse_all
    kp_ref[...] = kall[subs[-1]]
    vp_ref[...] = vall[subs[-1]]


def _attn_prompt_group(q, kv, gi):
    batch, dil, ls, _ = q.shape
    rows = BAND * min(ATTN_MAX_SUBTILES, ls // BAND)
    assert ls % rows == 0, (ls, rows)
    tile = pl.BlockSpec((None, None, rows, DIL_DIM), lambda b, r, u: (b, r, u, 0))
    v_tile = pl.BlockSpec((None, None, rows, DIL_DIM), lambda b, r, u: (b, r, u, 1))
    return pl.pallas_call(
        _attn_prompt_kernel,
        grid=(batch, dil, ls // rows),
        in_specs=[tile, tile, v_tile],
        out_specs=[tile, pl.BlockSpec((None, None, rows, DIL_HD), lambda b, r, u: (b, r, u, 0))],
        out_shape=[
            jax.ShapeDtypeStruct((batch, dil, ls, DIL_DIM), F32),
            jax.ShapeDtypeStruct((batch, dil, ls, DIL_HD), F32),
        ],
        scratch_shapes=[pltpu.VMEM((BAND, DIL_DIM), BF16), pltpu.VMEM((BAND, DIL_DIM), BF16)],
        compiler_params=_params("parallel", "parallel", "arbitrary"),
        name="attn_prompt_g%d" % gi,
    )(q, kv, kv)


def _merge_matmul_res_kernel(o0, o1, o2, l0, l1, l2, w_ref, r_ref, out_ref, ot_ref, lt_ref):
    tm = out_ref.shape[0]
    sub = MERGE_SUB_ROWS
    wb = w_ref[...].astype(BF16)
    for s0 in range(0, tm, sub):
        rows = slice(s0, s0 + sub)
        for gi, (o_ref, l_ref) in enumerate(((o1, l1), (o2, l2))):
            dil = DIL_GROUPS[gi + 1][1]
            us = slice(s0 // dil, (s0 + sub) // dil)
            for r in range(dil):
                for h in range(DIL_HEADS):
                    ot_ref[gi, h, pl.ds(s0 + r, sub // dil, stride=dil), :] = o_ref[
                        r, us, h * DIL_HD:(h + 1) * DIL_HD]
                lt_ref[gi, pl.ds(s0 + r, sub // dil, stride=dil), :] = l_ref[r, us, :]
        a0, a1, a2 = l0[0, rows, :], lt_ref[0, rows, :], lt_ref[1, rows, :]
        m = jnp.maximum(jnp.maximum(a0, a1), a2)
        e0, e1, e2 = jnp.exp(a0 - m), jnp.exp(a1 - m), jnp.exp(a2 - m)
        inv = 1.0 / (e0 + e1 + e2)
        w0, w1, w2 = e0 * inv, e1 * inv, e2 * inv
        parts = []
        for h in range(DIL_HEADS):
            sl = slice(h * DIL_HD, (h + 1) * DIL_HD)
            parts.append((o0[0, rows, sl] * w0[:, h:h + 1] + ot_ref[0, h, rows, :] * w1[:, h:h + 1]
                          + ot_ref[1, h, rows, :] * w2[:, h:h + 1]).astype(BF16))
        out_ref[rows, :] = r_ref[rows, :] + _dot(jnp.concatenate(parts, axis=1), wb)


def _merge_matmul_res(os, lses, w_arr, layer, res, seq, tm):
    m = res.shape[0]
    nb = seq // tm
    in_specs = []
    for width in (DIL_DIM, DIL_HD):
        for _, dil in DIL_GROUPS:
            in_specs.append(pl.BlockSpec((None, dil, tm // dil, width),
                                         lambda i: (i // nb, 0, i % nb, 0)))
    in_specs += [
        pl.BlockSpec((None, DIL_DIM, D_MODEL), lambda i: (layer, 0, 0), pipeline_mode=pl.Buffered(1)),
        pl.BlockSpec((tm, D_MODEL), lambda i: (i, 0)),
    ]
    return pl.pallas_call(
        _merge_matmul_res_kernel,
        grid=(m // tm,),
        in_specs=in_specs,
        out_specs=pl.BlockSpec((tm, D_MODEL), lambda i: (i, 0)),
        out_shape=jax.ShapeDtypeStruct((m, D_MODEL), F32),
        scratch_shapes=[
            pltpu.VMEM((N_GROUPS - 1, DIL_HEADS, tm, DIL_HD), F32),
            pltpu.VMEM((N_GROUPS - 1, tm, DIL_HD), F32),
        ],
        compiler_params=_params("parallel"),
        name="merge_matmul_res",
    )(*os, *lses, w_arr, res)


def _attn_sample_kernel(q_ref, kvn_ref, c0_ref, c1_ref, c2_ref, o_ref):
    caches = (c0_ref, c1_ref, c2_ref)
    steps = float(BAND) - lax.broadcasted_iota(jnp.int32, (BAND, 1, 1), 0).astype(F32)
    head = lax.broadcasted_iota(jnp.int32, (1, DIL_HEADS, 1), 1)
    slope = jnp.zeros((1, DIL_HEADS, 1), F32)
    for h in range(DIL_HEADS):
        slope = jnp.where(head == h, 2.0 ** -(h + 1), slope)
    bias = slope * steps
    outs, lses = [], []
    for g in range(N_GROUPS):
        c_ref = caches[g]
        q = q_ref[g]
        kn, vn = kvn_ref[g, 0], kvn_ref[g, 1]
        sb = jnp.sum(c_ref[:, 0] * q[None], axis=-1, keepdims=True) - bias
        sn = jnp.sum(kn * q, axis=-1, keepdims=True)
        m = jnp.maximum(jnp.max(sb, axis=0), sn)
        pb = jnp.exp(sb - m[None])
        pn = jnp.exp(sn - m)
        l = jnp.sum(pb, axis=0) + pn
        outs.append((jnp.sum(pb * c_ref[:, 1], axis=0) + pn * vn) / l)
        lses.append(m + jnp.log(l))
    m = jnp.maximum(jnp.maximum(lses[0], lses[1]), lses[2])
    es = [jnp.exp(a - m) for a in lses]
    inv = 1.0 / (es[0] + es[1] + es[2])
    o_ref[...] = outs[0] * (es[0] * inv) + outs[1] * (es[1] * inv) + outs[2] * (es[2] * inv)


def _attn_sample(q, kv_new, caches):
    b = q.shape[0]
    views, cache_specs = [], []
    for cache, (_, dil) in zip(caches, DIL_GROUPS):
        wb = cache.shape[1]
        views.append(cache.reshape(b, wb // dil, dil, 2, DIL_HEADS, DIL_HD))
        cache_specs.append(pl.BlockSpec((None, BAND, None, 2, DIL_HEADS, DIL_HD),
                                        lambda i: (i, 0, 0, 0, 0, 0)))
    o = pl.pallas_call(
        _attn_sample_kernel,
        grid=(b,),
        in_specs=[
            pl.BlockSpec((None, N_GROUPS, DIL_HEADS, DIL_HD), lambda i: (i, 0, 0, 0)),
            pl.BlockSpec((None, N_GROUPS, 2, DIL_HEADS, DIL_HD), lambda i: (i, 0, 0, 0, 0)),
        ] + cache_specs,
        out_specs=pl.BlockSpec((None, DIL_HEADS, DIL_HD), lambda i: (i, 0, 0)),
        out_shape=jax.ShapeDtypeStruct((b, DIL_HEADS, DIL_HD), F32),
        compiler_params=_params("parallel"),
        name="attn_sample",
    )(q.reshape(b, N_GROUPS, DIL_HEADS, DIL_HD),
      kv_new.reshape(b, N_GROUPS, 2, DIL_HEADS, DIL_HD), *views)
    return o.reshape(b, DIL_DIM)


def _trunk(x, xs, batch, seq, hg_state, caches, wts):
    (norm_g, ffn_w_in, ffn_w_out, hg_w_in, hg_w_out, lbs, hg_out_g, kv_norm_g, w_kv,
     k_norm_g, dil_w_q, dil_q_norm_g, dil_w_o) = wts
    ns = xs.shape[0]
    tm, tn = 512, 1024
    states_p, states_s = [], []
    kv_p = kv_s = None
    gain = lambda layer, idx: pl.BlockSpec((None, None, 1, D_MODEL), lambda i, j: (layer, idx, 0, 0))
    to_cast = [(hg_w_in, (layer,)) for layer in range(N_A_LAYERS)] + [(w_kv, ())]
    casted = []

    def ffn(x, xs, layer, half, gain_idx):
        cast = to_cast[len(casted)] if len(casted) < len(to_cast) else None
        outs = _ffn(x, xs, norm_g, ffn_w_in, ffn_w_out, layer, half, gain_idx, cast)
        if cast is not None:
            casted.append(outs[2])
        return outs[0], outs[1]

    for layer in range(DEPTH):
        x, xs = ffn(x, xs, layer, 0, 0)
        if layer < N_A_LAYERS:
            hg_w = casted[layer]
            w_spec = pl.BlockSpec((D_MODEL, tn), lambda i, j: (0, j))
            lb = lbs[layer].reshape(HG_HEADS, HG_DK)
            g_out = hg_out_g[layer].reshape(1, HG_DK)
            wide_spec = pl.BlockSpec((D_MODEL, 2 * tn), lambda i, j: (0, j))
            proj = _norm_matmul(x, norm_g, gain(layer, 1), hg_w, wide_spec,
                                4 * D_MODEL, k_norm_g, "plain", 2 * tm, 2 * tn)
            o, s = _hgrn_scan(proj, lb, g_out, batch, seq)
            states_p.append(s)
            x = _matmul_res(o, hg_w_out, layer, x, tm)
            proj = _norm_matmul(xs, norm_g, gain(layer, 1), hg_w, w_spec,
                                4 * D_MODEL, k_norm_g, "plain", ns, tn)
            o, s = _hgrn_step(proj, hg_state, layer, lb, g_out)
            states_s.append(s)
            xs = _matmul_res(o, hg_w_out, layer, xs, ns)
        else:
            bi = layer - N_A_LAYERS
            qs = _q_streams(x, norm_g,
                            pl.BlockSpec((None, None, 1, D_MODEL), lambda i, layer=layer: (layer, 1, 0, 0)),
                            dil_w_q, bi, dil_q_norm_g[bi], batch, seq, tm)
            parts = [_attn_prompt_group(qs[gi], kv_p[gi], gi) for gi in range(N_GROUPS)]
            x = _merge_matmul_res([p[0] for p in parts], [p[1] for p in parts],
                                  dil_w_o, bi, x, seq, tm)
            w_spec = pl.BlockSpec((None, D_MODEL, DIL_DIM), lambda i, j, bi=bi: (bi, 0, j))
            q = _norm_matmul(xs, norm_g, gain(layer, 1), dil_w_q, w_spec,
                             N_GROUPS * DIL_DIM, dil_q_norm_g[bi], "q", ns, DIL_DIM)
            xs = _matmul_res(_attn_sample(q, kv_s, caches), dil_w_o, bi, xs, ns)
        x, xs = ffn(x, xs, layer, 1, 2)
        if layer == N_A_LAYERS - 1:
            kv_gain = kv_norm_g.reshape(1, D_MODEL)
            kv_gain_spec = pl.BlockSpec((1, D_MODEL), lambda i, j: (0, 0))
            w_spec = pl.BlockSpec((D_MODEL, DIL_DIM), lambda i, j: (0, j))
            kv_w = casted[N_A_LAYERS]
            kv_p = _kv_streams(x, kv_gain, kv_w, k_norm_g, batch, seq, tm // 2)
            kv_s = _norm_matmul(xs, kv_gain, kv_gain_spec, kv_w, w_spec,
                                2 * N_GROUPS * DIL_DIM, k_norm_g, "kv", ns, DIL_DIM)
    return x, xs, jnp.stack(states_p, axis=0), jnp.stack(states_s, axis=0), kv_p, kv_s


def kernel(x_prompt, x_sample, state_hgrn, cache_kv_w128, cache_kv_w512, cache_kv_w2048,
           norm_g, ffn_w_in, ffn_w_out, hg_w_in, hg_w_out, hg_lb_logits, hg_out_g,
           kv_norm_g, w_kv, k_norm_g, dil_w_q, dil_q_norm_g, dil_w_o):
    batch, seq, _ = x_prompt.shape
    dec_batch = x_sample.shape[0]
    p = jax.nn.softmax(hg_lb_logits.astype(F32), axis=0)
    lbs = jnp.cumsum(p, axis=0) - p[0]
    wts = (norm_g.reshape(DEPTH, 3, 1, D_MODEL), ffn_w_in, ffn_w_out,
           hg_w_in, hg_w_out, lbs, hg_out_g, kv_norm_g,
           w_kv, k_norm_g, dil_w_q, dil_q_norm_g, dil_w_o)

    y_p, y_s, hg_p, hg_s, kv_p, kv_s = _trunk(
        x_prompt.reshape(batch * seq, D_MODEL), x_sample.reshape(dec_batch, D_MODEL), batch, seq,
        state_hgrn, (cache_kv_w128, cache_kv_w512, cache_kv_w2048), wts)

    kv_s = kv_s.reshape(dec_batch, 1, N_GROUPS, 2, DIL_HEADS, DIL_HD)
    outs = [y_p.reshape(batch, seq, D_MODEL), y_s.reshape(dec_batch, 1, D_MODEL), hg_p, hg_s]
    for gi, (win, dil) in enumerate(DIL_GROUPS):
        rows = min(win, seq)
        tail = jnp.swapaxes(kv_p[gi][:, :, (seq - rows) // dil:], 1, 2)
        outs.append(tail.reshape(batch, rows, 2, DIL_HEADS, DIL_HD))
        outs.append(kv_s[:, :, gi])
    return tuple(outs)
```
